```python
import math
import jax, jax.numpy as jnp
from jax import lax
import numpy as np

D_MODEL = 1024
BATCH = 16
SEQ = 2048
DEPTH = 4

HEAD_DIM = 64
N_HEADS_DIL = 6
N_HEADS_DSA = 4
N_HEADS_MLA = 6
MIX_WIDTH = (N_HEADS_DIL + N_HEADS_DSA + N_HEADS_MLA) * HEAD_DIM
DILATIONS = ((128, 1), (512, 4), (2048, 16))
IDX_HEADS = 8
IDX_DIM = 64
DSA_TOPK = 256
Q_BLOCK = 128
MLA_Q_RANK = 256
MLA_KV_RANK = 128
MLA_NOPE = 64
MLA_ROPE = 32
MLA_V = 64
ROPE_THETA = 10000.0
N_BUCKETS = 32
MAX_DISTANCE = 2048
N_BIAS_HEADS = N_HEADS_DIL + N_HEADS_DSA
N_EXPERTS = 64
TOP_K = 8
EXPERT_FF = 256
SHARED_FF = 256
ROUTED_SCALE = 2.5
MOE_BLOCK = 128
DN_ALPHA = (2 * DEPTH) ** 0.25
DN_BETA = (8 * DEPTH) ** -0.25
LN_EPS = 1e-5
RMS_EPS = 1e-6

IN_SIZES = (
    N_HEADS_DIL * HEAD_DIM, N_HEADS_DIL * HEAD_DIM, N_HEADS_DIL * HEAD_DIM,
    N_HEADS_DSA * HEAD_DIM, N_HEADS_DSA * HEAD_DIM, N_HEADS_DSA * HEAD_DIM,
    IDX_HEADS * IDX_DIM, IDX_DIM, IDX_HEADS,
    MLA_Q_RANK, MLA_KV_RANK, MLA_ROPE,
)
VALUE_SEGMENTS = (2, 5)
IN_WIDTH = sum(IN_SIZES)

kernel_name = "hybrid_dilated_dsa_mla_moe"

F32 = jnp.float32


def layer_norm(x, g, b):
    xf = x.astype(F32)
    mu = jnp.mean(xf, -1, keepdims=True)
    var = jnp.mean(jnp.square(xf - mu), -1, keepdims=True)
    return ((xf - mu) * lax.rsqrt(var + LN_EPS) * g + b).astype(x.dtype)


def rms_norm(x, g):
    xf = x.astype(F32)
    return (xf * lax.rsqrt(jnp.mean(jnp.square(xf), -1, keepdims=True) + RMS_EPS) * g).astype(x.dtype)


def t5_bucket(dist):
    max_exact = N_BUCKETS // 2
    d_f = jnp.maximum(dist, 1).astype(F32)
    large = max_exact + (jnp.log(d_f / max_exact) / math.log(MAX_DISTANCE / max_exact)
                         * (N_BUCKETS - max_exact)).astype(jnp.int32)
    large = jnp.minimum(large, N_BUCKETS - 1)
    return jnp.where(dist < max_exact, dist, large)


def apply_rope(x, cos, sin):
    half = x.shape[-1] // 2
    x1, x2 = x[..., :half].astype(F32), x[..., half:].astype(F32)
    return jnp.concatenate([x1 * cos - x2 * sin, x2 * cos + x1 * sin], -1).astype(x.dtype)


def split_cols(u):
    out, start = [], 0
    for n in IN_SIZES:
        out.append(u[..., start:start + n])
        start += n
    return out


def dilated_branch(q, k, v, dilation, span, table):
    B, S, H, E = q.shape
    Ls = S // dilation
    nb = -(-Ls // span)
    Lp = nb * span

    def strided(t):
        t = t.reshape(B, Ls, dilation, H, E).transpose(0, 2, 1, 3, 4)
        t = jnp.pad(t, ((0, 0), (0, 0), (0, Lp - Ls), (0, 0), (0, 0)))
        return t.reshape(B, dilation, nb, span, H, E)

    def band(t):
        prev = jnp.pad(t[:, :, :-1], ((0, 0), (0, 0), (1, 0), (0, 0), (0, 0), (0, 0)))
        return jnp.concatenate([prev, t], axis=3)

    qs = strided(q)
    kb, vb = band(strided(k)), band(strided(v))
    qi = jnp.arange(span)[:, None]
    kj = jnp.arange(2 * span)[None, :]
    delta = span + qi - kj
    in_band = (delta >= 0) & (delta <= span)
    has_prev = (jnp.arange(nb)[:, None, None] > 0) | (kj >= span)[None]
    valid = in_band[None] & has_prev
    bias = table[t5_bucket(jnp.maximum(delta, 0) * dilation)].transpose(2, 0, 1)
    logits = jnp.einsum('brnqhe,brnkhe->brnhqk', qs, kb, preferred_element_type=F32) * (HEAD_DIM ** -0.5)
    logits = jnp.where(valid[:, None], logits + bias, -jnp.inf)
    lse = jax.nn.logsumexp(logits, axis=-1)
    p = jnp.exp(logits - lse[..., None])
    o = jnp.einsum('brnhqk,brnkhe->brnqhe', p.astype(v.dtype), vb, preferred_element_type=F32)

    def unstride(t):
        t = t.reshape(B, dilation, Lp, *t.shape[4:])[:, :, :Ls]
        return jnp.swapaxes(t, 1, 2).reshape(B, S, *t.shape[3:])

    return unstride(o), unstride(jnp.swapaxes(lse, 3, 4))


def dilated_mixture(q, k, v, table):
    outs, lses = [], []
    for window, dilation in DILATIONS:
        o, l = dilated_branch(q, k, v, dilation, window // dilation, table)
        outs.append(o)
        lses.append(l)
    wts = jax.nn.softmax(jnp.stack(lses), axis=0)
    return jnp.einsum('gbsh,gbshe->bshe', wts, jnp.stack(outs)).astype(q.dtype)


def dsa_attention(q, k, v, iq, ik, iw, table):
    B, S, H, E = q.shape
    n_sel = min(DSA_TOPK, S // 4)
    nq = S // Q_BLOCK
    key_pos = jnp.arange(S)
    gather = jax.vmap(lambda t, i: t[i])

    def to_blocks(t):
        return jnp.swapaxes(t.reshape(B, nq, Q_BLOCK, *t.shape[2:]), 0, 1)

    def block(args):
        qb, iqb, iwb, start = args
        t = start + jnp.arange(Q_BLOCK)
        causal = key_pos[None, :] <= t[:, None]
        rel = jax.nn.relu(jnp.einsum('bqhe,bse->bqhs', iqb, ik, preferred_element_type=F32))
        score = jnp.einsum('bqh,bqhs->bqs', iwb.astype(F32), rel)
        score = jnp.where(causal[None], score, -jnp.inf)
        _, idx = lax.top_k(score, n_sel)
        valid = idx <= t[None, :, None]
        ksel, vsel = gather(k, idx), gather(v, idx)
        bias = table[t5_bucket(jnp.maximum(t[None, :, None] - idx, 0))]
        logits = jnp.einsum('bqhe,bqkhe->bqhk', qb, ksel, preferred_element_type=F32) * (HEAD_DIM ** -0.5)
        logits = jnp.where(valid[:, :, None, :], logits + jnp.swapaxes(bias, 2, 3), -jnp.inf)
        p = jax.nn.softmax(logits, axis=-1)
        return jnp.einsum('bqhk,bqkhe->bqhe', p.astype(v.dtype), vsel, preferred_element_type=F32)

    out = lax.map(block, (to_blocks(q), to_blocks(iq), to_blocks(iw), jnp.arange(nq) * Q_BLOCK))
    return jnp.swapaxes(out, 0, 1).reshape(B, S, H, E).astype(q.dtype)


def mla_attention(cq, ckv, k_rope, positions, q_norm, w_uq, kv_norm, w_ukv):
    B, S, _ = cq.shape
    q = (rms_norm(cq, q_norm) @ w_uq).reshape(B, S, N_HEADS_MLA, MLA_NOPE + MLA_ROPE)
    kv = (rms_norm(ckv, kv_norm) @ w_ukv).reshape(B, S, N_HEADS_MLA, MLA_NOPE + MLA_V)
    inv = ROPE_THETA ** (-jnp.arange(0, MLA_ROPE, 2, dtype=F32) / MLA_ROPE)
    ang = positions.astype(F32)[..., None] * inv
    cos, sin = jnp.cos(ang), jnp.sin(ang)
    q_nope = q[..., :MLA_NOPE]
    q_rope = apply_rope(q[..., MLA_NOPE:], cos[:, :, None], sin[:, :, None])
    k_nope, v = kv[..., :MLA_NOPE], kv[..., MLA_NOPE:]
    k_r = apply_rope(k_rope, cos, sin)
    scale = (MLA_NOPE + MLA_ROPE) ** -0.5
    nq = S // Q_BLOCK
    key_pos = jnp.arange(S)

    def to_blocks(t):
        return jnp.swapaxes(t.reshape(B, nq, Q_BLOCK, *t.shape[2:]), 0, 1)

    def block(args):
        qn, qr, start = args
        t = start + jnp.arange(Q_BLOCK)
        logits = (jnp.einsum('bqhe,bshe->bhqs', qn, k_nope, preferred_element_type=F32)
                  + jnp.einsum('bqhe,bse->bhqs', qr, k_r, preferred_element_type=F32)) * scale
        logits = jnp.where(key_pos[None, :] <= t[:, None], logits, -jnp.inf)
        p = jax.nn.softmax(logits, axis=-1)
        return jnp.einsum('bhqs,bshe->bqhe', p.astype(v.dtype), v, preferred_element_type=F32)

    out = lax.map(block, (to_blocks(q_nope), to_blocks(q_rope), jnp.arange(nq) * Q_BLOCK))
    return jnp.swapaxes(out, 0, 1).reshape(B, S, N_HEADS_MLA, MLA_V).astype(cq.dtype)


def hybrid_mixer(x, positions, w_in, q_norm, w_uq, kv_norm, w_ukv, w_out, rel_bias):
    B, S, _ = x.shape
    dq, dk, dv, sq, sk, sv, iq, ik, iw, cq, ckv, kr = split_cols(x @ w_in)
    heads = lambda t, h: t.reshape(B, S, h, -1)
    o_a = dilated_mixture(heads(dq, N_HEADS_DIL), heads(dk, N_HEADS_DIL), heads(dv, N_HEADS_DIL),
                          rel_bias[:, :N_HEADS_DIL])
    o_b = dsa_attention(heads(sq, N_HEADS_DSA), heads(sk, N_HEADS_DSA), heads(sv, N_HEADS_DSA),
                        heads(iq, IDX_HEADS), ik, iw, rel_bias[:, N_HEADS_DIL:])
    o_c = mla_attention(cq, ckv, kr, positions, q_norm, w_uq, kv_norm, w_ukv)
    o = jnp.concatenate([o_a.reshape(B, S, -1), o_b.reshape(B, S, -1), o_c.reshape(B, S, -1)], -1)
    return o.astype(x.dtype) @ w_out


def moe_ffn(x, router_w, router_bias, w1, w3, w2, sw1, sw3, sw2):
    B, S, D = x.shape
    xf = x.reshape(-1, D)
    T = xf.shape[0]
    scores = jax.nn.sigmoid(jnp.matmul(xf, router_w, preferred_element_type=F32))
    _, idx = lax.top_k(scores + router_bias.astype(F32), TOP_K)
    gates = jnp.take_along_axis(scores, idx, -1)
    gates = gates / jnp.sum(gates, -1, keepdims=True) * ROUTED_SCALE
    n_assign = T * TOP_K
    e_flat = idx.reshape(-1)
    tok_flat = jnp.repeat(jnp.arange(T, dtype=jnp.int32), TOP_K)
    g_flat = gates.reshape(-1)
    order = jnp.argsort(e_flat)
    e_s, tok_s, g_s = e_flat[order], tok_flat[order], g_flat[order]
    counts = jnp.bincount(e_flat, length=N_EXPERTS)
    starts = jnp.cumsum(counts) - counts
    pcounts = (counts + MOE_BLOCK - 1) // MOE_BLOCK * MOE_BLOCK
    pstarts = jnp.cumsum(pcounts) - pcounts
    dest = pstarts[e_s] + (jnp.arange(n_assign) - starts[e_s])
    n_blocks = -(-n_assign // MOE_BLOCK) + N_EXPERTS
    n_rows = n_blocks * MOE_BLOCK
    row_tok = jnp.zeros((n_rows,), jnp.int32).at[dest].set(tok_s)
    row_gate = jnp.zeros((n_rows,), F32).at[dest].set(g_s)
    block_end = jnp.cumsum(pcounts) // MOE_BLOCK
    block_expert = jnp.minimum(jnp.searchsorted(block_end, jnp.arange(n_blocks), side='right'), N_EXPERTS - 1)

    def expert_block(args):
        tok, gate, e = args
        h = xf[tok]
        y = (jax.nn.silu(h @ w1[e]) * (h @ w3[e])) @ w2[e]
        return y.astype(F32) * gate[:, None]

    ys = lax.map(expert_block, (row_tok.reshape(n_blocks, MOE_BLOCK), row_gate.reshape(n_blocks, MOE_BLOCK),
                                block_expert))
    routed = jnp.zeros((T, D), F32).at[row_tok].add(ys.reshape(n_rows, D))
    shared = (jax.nn.silu(xf @ sw1) * (xf @ sw3)) @ sw2
    return (routed + shared.astype(F32)).astype(x.dtype).reshape(B, S, D)


def setup_inputs(seed: int = 0) -> dict:
    key = jax.random.key(seed)
    ks = jax.random.split(key, 24)
    nrm = lambda k, s: jax.random.normal(k, s, F32)
    col_scale = jnp.concatenate([jnp.full((n,), DN_BETA if i in VALUE_SEGMENTS else 1.0, F32)
                                 for i, n in enumerate(IN_SIZES)])
    ukv_scale = jnp.tile(jnp.concatenate([jnp.ones((MLA_NOPE,), F32), jnp.full((MLA_V,), DN_BETA, F32)]),
                         N_HEADS_MLA)
    return {
        "x": nrm(ks[0], (BATCH, SEQ, D_MODEL)),
        "positions": jnp.broadcast_to(jnp.arange(SEQ, dtype=jnp.int32), (BATCH, SEQ)),
        "w_in": nrm(ks[1], (DEPTH, D_MODEL, IN_WIDTH)) * D_MODEL ** -0.5 * col_scale,
        "mla_q_norm": 1.0 + 0.02 * nrm(ks[2], (DEPTH, MLA_Q_RANK)),
        "mla_w_uq": nrm(ks[3], (DEPTH, MLA_Q_RANK, N_HEADS_MLA * (MLA_NOPE + MLA_ROPE))) * MLA_Q_RANK ** -0.5,
        "mla_kv_norm": 1.0 + 0.02 * nrm(ks[4], (DEPTH, MLA_KV_RANK)),
        "mla_w_ukv": nrm(ks[5], (DEPTH, MLA_KV_RANK, N_HEADS_MLA * (MLA_NOPE + MLA_V))) * MLA_KV_RANK ** -0.5 * ukv_scale,
        "w_out": nrm(ks[6], (DEPTH, MIX_WIDTH, D_MODEL)) * MIX_WIDTH ** -0.5 * DN_BETA,
        "ln1_g": 1.0 + 0.02 * nrm(ks[7], (DEPTH, D_MODEL)),
        "ln1_b": 0.02 * nrm(ks[8], (DEPTH, D_MODEL)),
        "router_w": nrm(ks[9], (DEPTH, D_MODEL, N_EXPERTS)) * D_MODEL ** -0.5,
        "router_bias": 0.01 * nrm(ks[10], (DEPTH, N_EXPERTS)),
        "exp_w1": nrm(ks[11], (DEPTH, N_EXPERTS, D_MODEL, EXPERT_FF)) * D_MODEL ** -0.5,
        "exp_w3": nrm(ks[12], (DEPTH, N_EXPERTS, D_MODEL, EXPERT_FF)) * D_MODEL ** -0.5,
        "exp_w2": nrm(ks[13], (DEPTH, N_EXPERTS, EXPERT_FF, D_MODEL)) * EXPERT_FF ** -0.5 * DN_BETA,
        "sh_w1": nrm(ks[14], (DEPTH, D_MODEL, SHARED_FF)) * D_MODEL ** -0.5,
        "sh_w3": nrm(ks[15], (DEPTH, D_MODEL, SHARED_FF)) * D_MODEL ** -0.5,
        "sh_w2": nrm(ks[16], (DEPTH, SHARED_FF, D_MODEL)) * SHARED_FF ** -0.5 * DN_BETA,
        "ln2_g": 1.0 + 0.02 * nrm(ks[17], (DEPTH, D_MODEL)),
        "ln2_b": 0.02 * nrm(ks[18], (DEPTH, D_MODEL)),
        "rel_bias": 0.3 * nrm(ks[19], (N_BUCKETS, N_BIAS_HEADS)),
    }


def reference(x, positions, w_in, mla_q_norm, mla_w_uq, mla_kv_norm, mla_w_ukv, w_out, ln1_g, ln1_b,
              router_w, router_bias, exp_w1, exp_w3, exp_w2, sh_w1, sh_w3, sh_w2, ln2_g, ln2_b, rel_bias):
    for l in range(DEPTH):
        mix = hybrid_mixer(x, positions, w_in[l], mla_q_norm[l], mla_w_uq[l], mla_kv_norm[l], mla_w_ukv[l],
                           w_out[l], rel_bias)
        x = layer_norm(DN_ALPHA * x + mix, ln1_g[l], ln1_b[l])
        ffn = moe_ffn(x, router_w[l], router_bias[l], exp_w1[l], exp_w3[l], exp_w2[l],
                      sh_w1[l], sh_w3[l], sh_w2[l])
        x = layer_norm(DN_ALPHA * x + ffn, ln2_g[l], ln2_b[l])
    return x
```

```python
import functools
import math

import jax
import jax.numpy as jnp
import numpy as np
from jax import lax
from jax.experimental import pallas as pl
from jax.experimental.pallas import tpu as pltpu

F32 = jnp.float32
BF16 = jnp.bfloat16

D_MODEL = 1024
HEAD_DIM = 64
N_HEADS_DIL = 6
N_HEADS_DSA = 4
N_HEADS_MLA = 6
DILATIONS = ((128, 1), (512, 4), (2048, 16))
IDX_HEADS = 8
IDX_DIM = 64
DSA_TOPK = 256
MLA_Q_RANK = 256
MLA_KV_RANK = 128
MLA_NOPE = 64
MLA_ROPE = 32
MLA_V = 64
ROPE_THETA = 10000.0
N_BUCKETS = 32
MAX_DISTANCE = 2048
N_EXPERTS = 64
TOP_K = 8
EXPERT_FF = 256
ROUTED_SCALE = 2.5
LN_EPS = 1e-5
RMS_EPS = 1e-6

LANES = 128
NEG = -1e30
INT_MIN = -2147483648
VMEM_LIMIT = 56 * 1024 * 1024
BQ = 256

WA_COLS = 3 * N_HEADS_DIL * HEAD_DIM + 3 * N_HEADS_DSA * HEAD_DIM + IDX_HEADS * IDX_DIM + 2 * IDX_DIM
WB_COLS = MLA_Q_RANK + MLA_KV_RANK + LANES
IQ_BLK = 0
SQ_BLK, SK_BLK, SV_BLK = 4, 6, 8
IK_BLK = 10
DQ_BLK, DK_BLK, DV_BLK = 11, 14, 17


def _cparams(*sem):
    return pltpu.CompilerParams(dimension_semantics=sem, vmem_limit_bytes=VMEM_LIMIT)


def _mm_kernel(x_ref, w_ref, o_ref):
    o_ref[...] = jnp.dot(x_ref[...].astype(BF16), w_ref[...], preferred_element_type=F32).astype(o_ref.dtype)


def _matmul(x, w, out_dtype, tm=512):
    T, K = x.shape
    N = w.shape[1]
    return pl.pallas_call(
        _mm_kernel,
        grid=(T // tm,),
        in_specs=[pl.BlockSpec((tm, K), lambda i: (i, 0)), pl.BlockSpec((K, N), lambda i: (0, 0))],
        out_specs=pl.BlockSpec((tm, N), lambda i: (i, 0)),
        out_shape=jax.ShapeDtypeStruct((T, N), out_dtype),
        compiler_params=_cparams("parallel"),
        name="in_proj",
    )(x, w)


def _head_masks():
    lane = lax.broadcasted_iota(jnp.int32, (1, LANES), 1)
    return lane < HEAD_DIM, lane >= HEAD_DIM


def _select_head(q2, head_lanes):
    return jnp.where(head_lanes, q2 * jnp.asarray(HEAD_DIM ** -0.5, BF16), jnp.zeros((), BF16))


def _softmax_pv(s, v):
    m = jnp.max(s, axis=1, keepdims=True)
    p = jnp.exp(s - m)
    l = jnp.sum(p, axis=1, keepdims=True)
    return jnp.dot(p.astype(BF16), v, preferred_element_type=F32) / l


def _nt_dot(a, b):
    return lax.dot_general(a, b, (((1,), (1,)), ((), ())), preferred_element_type=F32)


def _attn_kernel(q_ref, k_ref, v_ref, w_ref, o_ref, *, S, packed, per_head_w):
    lo, hi = _head_masks()
    for i in range(S // BQ):
        n = (i + 1) * BQ
        rows = slice(i * BQ, (i + 1) * BQ)
        v2 = v_ref[0, :n, :]
        outs = []
        for h in range(2):
            if packed:
                qh = _select_head(q_ref[0, rows, :], lo if h == 0 else hi)
                kh = k_ref[0, :n, :]
            else:
                qh = q_ref[0, rows, h * LANES:(h + 1) * LANES]
                kh = k_ref[0, :n, h * LANES:(h + 1) * LANES]
            s = _nt_dot(qh, kh) + w_ref[h if per_head_w else 0, :, S - n:]
            outs.append(_softmax_pv(s, v2))
        o_ref[0, rows, :] = jnp.where(lo, outs[0], outs[1]).astype(o_ref.dtype)


def _attention(q_arr, k_arr, v_arr, w, *, q_blk, k_blk, v_blk, n_pairs, packed, per_head_w, name):
    B, S, _ = q_arr.shape
    qk_w = LANES if packed else 2 * LANES
    wh = 2 if per_head_w else 1
    return pl.pallas_call(
        functools.partial(_attn_kernel, S=S, packed=packed, per_head_w=per_head_w),
        grid=(n_pairs, B),
        in_specs=[
            pl.BlockSpec((1, S, qk_w), lambda p, b: (b, 0, q_blk + p)),
            pl.BlockSpec((1, S, qk_w), lambda p, b: (b, 0, k_blk + p)),
            pl.BlockSpec((1, S, LANES), lambda p, b: (b, 0, v_blk + p)),
            pl.BlockSpec((wh, BQ, S), (lambda p, b: (p, 0, 0)) if per_head_w else (lambda p, b: (0, 0, 0))),
        ],
        out_specs=pl.BlockSpec((1, S, LANES), lambda p, b: (b, 0, p)),
        out_shape=jax.ShapeDtypeStruct((B, S, n_pairs * LANES), BF16),
        compiler_params=_cparams("parallel", "parallel"),
        name=name,
    )(q_arr, k_arr, v_arr, w)


def _count(mask):
    return jnp.sum(jnp.where(mask, 1.0, 0.0).astype(F32), axis=1, keepdims=True)


def _dsa_kernel(sq_ref, sk_ref, sv_ref, iq_ref, ik_ref, iw_ref, w_ref, o_ref, *, S, n_sel):
    lo, hi = _head_masks()
    idx_bits = (S - 1).bit_length()
    ksel = float(n_sel)
    for i in range(S // BQ):
        n = (i + 1) * BQ
        rows = slice(i * BQ, (i + 1) * BQ)
        ikn = ik_ref[0, :n, :]
        iwb = iw_ref[0, rows, :]
        score = jnp.zeros((BQ, n), F32)
        for hh in range(IDX_HEADS):
            c = hh // 2
            iqc = jnp.where(lo if hh % 2 == 0 else hi, iq_ref[0, rows, c * LANES:(c + 1) * LANES],
                            jnp.zeros((), BF16))
            score = score + iwb[:, hh:hh + 1] * jnp.maximum(_nt_dot(iqc, ikn), 0.0)
        col = lax.broadcasted_iota(jnp.int32, (BQ, n), 1)
        row = lax.broadcasted_iota(jnp.int32, (BQ, n), 0)
        bits = lax.bitcast_convert_type(score, jnp.int32)
        key = bits ^ ((bits >> 31) & jnp.int32(0x7FFFFFFF))
        key = jnp.where(col <= row + i * BQ, key, jnp.int32(INT_MIN))

        tau0 = jnp.where(_count(key >= 0) >= ksel, jnp.int32(0), jnp.int32(INT_MIN))

        def tau_step(it, tau, key=key):
            cand = tau | jnp.left_shift(jnp.int32(1), 30 - it)
            return jnp.where(_count(key >= cand) >= ksel, cand, tau)

        tau = lax.fori_loop(0, 31, tau_step, tau0)
        gt = key > tau
        eq = key == tau
        need = ksel - _count(gt)

        def idx_step(it, j, eq=eq, col=col, need=need):
            cand = j + jnp.left_shift(jnp.int32(1), idx_bits - 1 - it)
            return jnp.where(_count(eq & (col < cand)) < need, cand, j)

        jmax = lax.fori_loop(0, idx_bits, idx_step, jnp.zeros((BQ, 1), jnp.int32))
        madd = jnp.where(gt | (eq & (col <= jmax)), 0.0, NEG).astype(F32)

        for p in range(N_HEADS_DSA // 2):
            q2 = sq_ref[0, rows, p * LANES:(p + 1) * LANES]
            k2 = sk_ref[0, :n, p * LANES:(p + 1) * LANES]
            v2 = sv_ref[0, :n, p * LANES:(p + 1) * LANES]
            outs = []
            for h in range(2):
                qh = _select_head(q2, lo if h == 0 else hi)
                s = _nt_dot(qh, k2) + w_ref[2 * p + h, :, S - n:] + madd
                outs.append(_softmax_pv(s, v2))
            o_ref[0, rows, p * LANES:(p + 1) * LANES] = jnp.where(lo, outs[0], outs[1]).astype(o_ref.dtype)


def _dsa(ua3, ub3, w):
    B, S, _ = ua3.shape
    n_sel = min(DSA_TOPK, S // 4)
    assert n_sel <= BQ
    hw = N_HEADS_DSA * HEAD_DIM
    return pl.pallas_call(
        functools.partial(_dsa_kernel, S=S, n_sel=n_sel),
        grid=(B,),
        in_specs=[
            pl.BlockSpec((1, S, hw), lambda b: (b, 0, SQ_BLK * LANES // hw)),
            pl.BlockSpec((1, S, hw), lambda b: (b, 0, SK_BLK * LANES // hw)),
            pl.BlockSpec((1, S, hw), lambda b: (b, 0, SV_BLK * LANES // hw)),
            pl.BlockSpec((1, S, IDX_HEADS * IDX_DIM), lambda b: (b, 0, IQ_BLK * LANES // (IDX_HEADS * IDX_DIM))),
            pl.BlockSpec((1, S, LANES), lambda b: (b, 0, IK_BLK)),
            pl.BlockSpec((1, S, LANES), lambda b: (b, 0, (MLA_Q_RANK + MLA_KV_RANK) // LANES)),
            pl.BlockSpec((N_HEADS_DSA, BQ, S), lambda b: (0, 0, 0)),
        ],
        out_specs=pl.BlockSpec((1, S, hw), lambda b: (b, 0, 0)),
        out_shape=jax.ShapeDtypeStruct((B, S, hw), BF16),
        compiler_params=_cparams("parallel"),
        name="dsa_attn",
    )(ua3, ua3, ua3, ua3, ua3, ub3, w)


def _mla_prep_kernel(ub_ref, qn_ref, kvn_ref, wq_ref, wk_ref, wv_ref, c_ref, s1_ref, s2_ref,
                     q_ref, k_ref, v_ref, *, scale):
    ub = ub_ref[...]
    cq = ub[:, :MLA_Q_RANK]
    ckv = ub[:, MLA_Q_RANK:MLA_Q_RANK + MLA_KV_RANK]
    kr_chunk = ub[:, MLA_Q_RANK + MLA_KV_RANK:]

    def rms(t, g):
        return t * lax.rsqrt(jnp.mean(t * t, axis=-1, keepdims=True) + RMS_EPS) * g

    qn = rms(cq, qn_ref[...]).astype(BF16)
    kvn = rms(ckv, kvn_ref[...]).astype(BF16)
    q = jnp.dot(qn, wq_ref[...], preferred_element_type=F32)
    kn = jnp.dot(kvn, wk_ref[...], preferred_element_type=F32)
    v = jnp.dot(kvn, wv_ref[...], preferred_element_type=F32)
    c, s1, s2 = c_ref[...], s1_ref[...], s2_ref[...]

    def rope(t):
        return t * c + pltpu.roll(t, 16, 1) * s1 + pltpu.roll(t, LANES - 16, 1) * s2

    lane = lax.broadcasted_iota(jnp.int32, (1, LANES), 1)
    kr = jnp.where((lane >= MLA_NOPE) & (lane < MLA_NOPE + MLA_ROPE), rope(kr_chunk), 0.0)
    for h in range(N_HEADS_MLA):
        cols = slice(h * LANES, (h + 1) * LANES)
        q_ref[:, cols] = (rope(q[:, cols]) * scale).astype(BF16)
        k_ref[:, cols] = (kn[:, cols] + kr).astype(BF16)
    v_ref[...] = v.astype(BF16)


def _mla_prep(ub, qn, kvn, wq, wk, wv, c, s1, s2, tm=512):
    T = ub.shape[0]
    scale = (MLA_NOPE + MLA_ROPE) ** -0.5
    row = lambda i: (i, 0)
    fix = lambda i: (0, 0)
    hq = N_HEADS_MLA * LANES
    hv = N_HEADS_MLA * MLA_V
    return pl.pallas_call(
        functools.partial(_mla_prep_kernel, scale=scale),
        grid=(T // tm,),
        in_specs=[
            pl.BlockSpec((tm, WB_COLS), row),
            pl.BlockSpec((1, MLA_Q_RANK), fix), pl.BlockSpec((1, MLA_KV_RANK), fix),
            pl.BlockSpec((MLA_Q_RANK, hq), fix), pl.BlockSpec((MLA_KV_RANK, hq), fix),
            pl.BlockSpec((MLA_KV_RANK, hv), fix),
            pl.BlockSpec((tm, LANES), row), pl.BlockSpec((tm, LANES), row), pl.BlockSpec((tm, LANES), row),
        ],
        out_specs=[pl.BlockSpec((tm, hq), row), pl.BlockSpec((tm, hq), row), pl.BlockSpec((tm, hv), row)],
        out_shape=[jax.ShapeDtypeStruct((T, hq), BF16), jax.ShapeDtypeStruct((T, hq), BF16),
                   jax.ShapeDtypeStruct((T, hv), BF16)],
        compiler_params=_cparams("parallel"),
        name="mla_prep",
    )(ub, qn, kvn, wq, wk, wv, c, s1, s2)


def _layer_norm(y, g, b):
    mu = jnp.mean(y, axis=-1, keepdims=True)
    yc = y - mu
    var = jnp.mean(yc * yc, axis=-1, keepdims=True)
    return yc * lax.rsqrt(var + LN_EPS) * g + b


def _outproj_kernel(oa_ref, ob_ref, oc_ref, wa_ref, wb_ref, wc_ref, x_ref, g_ref, b_ref, o_ref, *, alpha):
    acc = jnp.dot(oa_ref[...], wa_ref[...], preferred_element_type=F32)
    acc += jnp.dot(ob_ref[...], wb_ref[...], preferred_element_type=F32)
    acc += jnp.dot(oc_ref[...], wc_ref[...], preferred_element_type=F32)
    o_ref[...] = _layer_norm(alpha * x_ref[...] + acc, g_ref[...], b_ref[...])


def _outproj_ln(oa, ob, oc, wa, wb, wc, x, g, b, alpha, tm=512):
    T, D = x.shape
    row = lambda i: (i, 0)
    fix = lambda i: (0, 0)
    return pl.pallas_call(
        functools.partial(_outproj_kernel, alpha=alpha),
        grid=(T // tm,),
        in_specs=[pl.BlockSpec((tm, oa.shape[1]), row), pl.BlockSpec((tm, ob.shape[1]), row),
                  pl.BlockSpec((tm, oc.shape[1]), row),
                  pl.BlockSpec(wa.shape, fix), pl.BlockSpec(wb.shape, fix), pl.BlockSpec(wc.shape, fix),
                  pl.BlockSpec((tm, D), row), pl.BlockSpec((1, D), fix), pl.BlockSpec((1, D), fix)],
        out_specs=pl.BlockSpec((tm, D), row),
        out_shape=jax.ShapeDtypeStruct((T, D), F32),
        compiler_params=_cparams("parallel"),
        name="out_proj_ln",
    )(oa, ob, oc, wa, wb, wc, x, g, b)


def _router_kernel(x_ref, rw_ref, rb_ref, g_ref):
    logits = jnp.dot(x_ref[...].astype(BF16), rw_ref[...], preferred_element_type=F32)
    scores = 1.0 / (1.0 + jnp.exp(-logits))
    lane = lax.broadcasted_iota(jnp.int32, logits.shape, 1)
    lane_f = lane.astype(F32)
    work = jnp.where(lane < N_EXPERTS, scores + rb_ref[...], NEG)
    sel = lane < 0
    for _ in range(TOP_K):
        m = jnp.max(work, axis=1, keepdims=True)
        first = jnp.min(jnp.where(work == m, lane_f, float(LANES)), axis=1, keepdims=True)
        pick = lane_f == first
        sel = sel | pick
        work = jnp.where(pick, NEG, work)
    gs = jnp.where(sel, scores, 0.0)
    gates = gs / jnp.sum(gs, axis=1, keepdims=True) * ROUTED_SCALE
    g_ref[...] = jnp.where(lane == N_EXPERTS, 1.0, gates)


def _router(x, rw, rb, tm=512):
    T, D = x.shape
    return pl.pallas_call(
        _router_kernel,
        grid=(T // tm,),
        in_specs=[pl.BlockSpec((tm, D), lambda i: (i, 0)), pl.BlockSpec((D, LANES), lambda i: (0, 0)),
                  pl.BlockSpec((1, LANES), lambda i: (0, 0))],
        out_specs=pl.BlockSpec((tm, LANES), lambda i: (i, 0)),
        out_shape=jax.ShapeDtypeStruct((T, LANES), F32),
        compiler_params=_cparams("parallel"),
        name="router",
    )(x, rw, rb)


def _moe_kernel(x_ref, g_ref, w1_ref, w3_ref, w2_ref, lg_ref, lb_ref, o_ref, acc_ref, xb_ref, *, alpha):
    e = pl.program_id(1)

    @pl.when(e == 0)
    def _():
        acc_ref[...] = jnp.zeros_like(acc_ref)
        xb_ref[...] = x_ref[...].astype(BF16)

    xb = xb_ref[...]
    h1 = jnp.dot(xb, w1_ref[0], preferred_element_type=F32)
    h3 = jnp.dot(xb, w3_ref[0], preferred_element_type=F32)
    lane = lax.broadcasted_iota(jnp.int32, g_ref.shape, 1)
    ge = jnp.sum(jnp.where(lane == e, g_ref[...], 0.0), axis=1, keepdims=True)
    h = h1 / (1.0 + jnp.exp(-h1)) * h3 * ge
    acc_ref[...] += jnp.dot(h.astype(BF16), w2_ref[0], preferred_element_type=F32)

    @pl.when(e == pl.num_programs(1) - 1)
    def _():
        o_ref[...] = _layer_norm(alpha * x_ref[...] + acc_ref[...], lg_ref[...], lb_ref[...])


def _moe(x, gates, w1, w3, w2, lg, lb, alpha, tm=1024):
    T, D = x.shape
    tm = min(tm, T)
    n_e, _, ff = w1.shape
    return pl.pallas_call(
        functools.partial(_moe_kernel, alpha=alpha),
        grid=(T // tm, n_e),
        in_specs=[pl.BlockSpec((tm, D), lambda i, e: (i, 0)), pl.BlockSpec((tm, LANES), lambda i, e: (i, 0)),
                  pl.BlockSpec((1, D, ff), lambda i, e: (e, 0, 0)), pl.BlockSpec((1, D, ff), lambda i, e: (e, 0, 0)),
                  pl.BlockSpec((1, ff, D), lambda i, e: (e, 0, 0)),
                  pl.BlockSpec((1, D), lambda i, e: (0, 0)), pl.BlockSpec((1, D), lambda i, e: (0, 0))],
        out_specs=pl.BlockSpec((tm, D), lambda i, e: (i, 0)),
        out_shape=jax.ShapeDtypeStruct((T, D), F32),
        scratch_shapes=[pltpu.VMEM((tm, D), F32), pltpu.VMEM((tm, D), BF16)],
        compiler_params=_cparams("parallel", "arbitrary"),
        name="moe_ffn",
    )(x, gates, w1, w3, w2, lg, lb)


def _t5_bucket(dist):
    max_exact = N_BUCKETS // 2
    d_f = jnp.maximum(dist, 1).astype(F32)
    large = max_exact + (jnp.log(d_f / max_exact) / math.log(MAX_DISTANCE / max_exact)
                         * (N_BUCKETS - max_exact)).astype(jnp.int32)
    large = jnp.minimum(large, N_BUCKETS - 1)
    return jnp.where(dist < max_exact, dist, large)


def _bias_tables(rel_bias, S):
    dist = np.arange(S)
    e = rel_bias[_t5_bucket(jnp.asarray(dist, jnp.int32))].T
    mult = sum(((dist % d == 0) & (dist <= w)).astype(np.float32) for w, d in DILATIONS)
    e_dil = jnp.where(mult > 0, e[:N_HEADS_DIL] + jnp.log(jnp.maximum(mult, 1.0)), NEG)
    e_all = jnp.concatenate([e_dil, e[N_HEADS_DIL:], jnp.zeros((1, S), F32)], 0)
    dd = np.arange(BQ)[:, None] - np.arange(S)[None, :] + (S - BQ)
    w = jnp.where(dd >= 0, e_all[:, np.maximum(dd, 0)], NEG).astype(F32)
    return w[:N_HEADS_DIL], w[N_HEADS_DIL:N_HEADS_DIL + N_HEADS_DSA], w[-1:]


def _rope_tables(positions):
    inv = ROPE_THETA ** (-jnp.arange(0, MLA_ROPE, 2, dtype=F32) / MLA_ROPE)
    ang = positions.astype(F32).reshape(-1, 1) * inv
    cos, sin = jnp.cos(ang), jnp.sin(ang)
    T, half = cos.shape
    one = jnp.ones((T, MLA_NOPE), F32)
    z = lambda n: jnp.zeros((T, n), F32)
    tail = LANES - MLA_NOPE - MLA_ROPE
    c = jnp.concatenate([one, cos, cos, jnp.ones((T, tail), F32)], 1)
    s1 = jnp.concatenate([z(MLA_NOPE + half), sin, z(tail)], 1)
    s2 = jnp.concatenate([z(MLA_NOPE), -sin, z(half + tail)], 1)
    return c, s1, s2


def _split_w_in(w):
    o = np.cumsum([0, 384, 384, 384, 256, 256, 256, 512, 64, 8, 256, 128, 32])
    ik = w[:, o[7]:o[8]]
    wa = jnp.concatenate([w[:, o[6]:o[7]], w[:, o[3]:o[6]], ik, ik, w[:, :o[3]]], 1)
    z = lambda n: jnp.zeros((w.shape[0], n), w.dtype)
    chunk = jnp.concatenate([w[:, o[8]:o[9]], z(MLA_NOPE - IDX_HEADS), w[:, o[11]:o[12]],
                             z(LANES - MLA_NOPE - MLA_ROPE)], 1)
    wb = jnp.concatenate([w[:, o[9]:o[11]], chunk], 1)
    return wa.astype(BF16), wb.astype(BF16)


def _split_mla_weights(w_uq, w_ukv):
    r = w_uq.shape[0]
    wq = w_uq.reshape(r, N_HEADS_MLA, MLA_NOPE + MLA_ROPE)
    wq = jnp.pad(wq, ((0, 0), (0, 0), (0, LANES - MLA_NOPE - MLA_ROPE))).reshape(r, N_HEADS_MLA * LANES)
    r = w_ukv.shape[0]
    wkv = w_ukv.reshape(r, N_HEADS_MLA, MLA_NOPE + MLA_V)
    wk = jnp.pad(wkv[:, :, :MLA_NOPE], ((0, 0), (0, 0), (0, LANES - MLA_NOPE))).reshape(r, N_HEADS_MLA * LANES)
    wv = wkv[:, :, MLA_NOPE:].reshape(r, N_HEADS_MLA * MLA_V)
    return wq.astype(BF16), wk.astype(BF16), wv.astype(BF16)


def kernel(x, positions, w_in, mla_q_norm, mla_w_uq, mla_kv_norm, mla_w_ukv, w_out, ln1_g, ln1_b, router_w,
           router_bias, exp_w1, exp_w3, exp_w2, sh_w1, sh_w3, sh_w2, ln2_g, ln2_b, rel_bias):
    B, S, D = x.shape
    T = B * S
    depth = w_in.shape[0]
    alpha = (2 * depth) ** 0.25
    w_dil, w_dsa, w_causal = _bias_tables(rel_bias, S)
    rope_c, rope_s1, rope_s2 = _rope_tables(positions)
    na, nb = N_HEADS_DIL * HEAD_DIM, (N_HEADS_DIL + N_HEADS_DSA) * HEAD_DIM
    xf = x.reshape(T, D)
    for l in range(depth):
        wa, wb = _split_w_in(w_in[l])
        ua = _matmul(xf, wa, BF16)
        ub = _matmul(xf, wb, F32)
        ua3 = ua.reshape(B, S, WA_COLS)
        o_a = _attention(ua3, ua3, ua3, w_dil, q_blk=DQ_BLK, k_blk=DK_BLK, v_blk=DV_BLK,
                         n_pairs=N_HEADS_DIL // 2, packed=True, per_head_w=True, name="dilated_attn")
        o_b = _dsa(ua3, ub.reshape(B, S, WB_COLS), w_dsa)
        wq, wk, wv = _split_mla_weights(mla_w_uq[l], mla_w_ukv[l])
        q, k, v = _mla_prep(ub, mla_q_norm[l][None], mla_kv_norm[l][None], wq, wk, wv, rope_c, rope_s1, rope_s2)
        o_c = _attention(q.reshape(B, S, -1), k.reshape(B, S, -1), v.reshape(B, S, -1), w_causal,
                         q_blk=0, k_blk=0, v_blk=0, n_pairs=N_HEADS_MLA // 2, packed=False, per_head_w=False,
                         name="mla_attn")
        wo = w_out[l].astype(BF16)
        x1 = _outproj_ln(o_a.reshape(T, -1), o_b.reshape(T, -1), o_c.reshape(T, -1),
                         wo[:na], wo[na:nb], wo[nb:], xf, ln1_g[l][None], ln1_b[l][None], alpha)
        rw = jnp.pad(router_w[l], ((0, 0), (0, LANES - N_EXPERTS))).astype(BF16)
        rb = jnp.pad(router_bias[l].astype(F32), (0, LANES - N_EXPERTS))[None]
        gates = _router(x1, rw, rb)
        w1 = jnp.concatenate([exp_w1[l], sh_w1[l][None]], 0).astype(BF16)
        w3 = jnp.concatenate([exp_w3[l], sh_w3[l][None]], 0).astype(BF16)
        w2 = jnp.concatenate([exp_w2[l], sh_w2[l][None]], 0).astype(BF16)
        xf = _moe(x1, gates, w1, w3, w2, ln2_g[l][None], ln2_b[l][None], alpha)
    return xf.reshape(B, S, D)
```

```python
import functools
import math

import jax
import jax.numpy as jnp
import numpy as np
from jax import lax
from jax.experimental import pallas as pl
from jax.experimental.pallas import tpu as pltpu

F32 = jnp.float32
BF16 = jnp.bfloat16

D_MODEL = 1024
HEAD_DIM = 64
N_HEADS_DIL = 6
N_HEADS_DSA = 4
N_HEADS_MLA = 6
DILATIONS = ((128, 1), (512, 4), (2048, 16))
IDX_HEADS = 8
IDX_DIM = 64
DSA_TOPK = 256
MLA_Q_RANK = 256
MLA_KV_RANK = 128
MLA_NOPE = 64
MLA_ROPE = 32
MLA_V = 64
ROPE_THETA = 10000.0
N_BUCKETS = 32
MAX_DISTANCE = 2048
N_EXPERTS = 64
TOP_K = 8
EXPERT_FF = 256
ROUTED_SCALE = 2.5
LN_EPS = 1e-5
RMS_EPS = 1e-6

LANES = 128
NEG = -1e30
INT_MIN = -2147483648
VMEM_LIMIT = 56 * 1024 * 1024
BQ = 256

WA_COLS = 3 * N_HEADS_DIL * HEAD_DIM + 3 * N_HEADS_DSA * HEAD_DIM + IDX_HEADS * IDX_DIM + 2 * IDX_DIM
WB_COLS = MLA_Q_RANK + MLA_KV_RANK + LANES
IQ_BLK = 0
SQ_BLK, SK_BLK, SV_BLK = 4, 6, 8
IK_BLK = 10
DQ_BLK, DK_BLK, DV_BLK = 11, 14, 17


def _cparams(*sem):
    return pltpu.CompilerParams(dimension_semantics=sem, vmem_limit_bytes=VMEM_LIMIT)


def _mm_kernel(x_ref, w_ref, o_ref):
    o_ref[...] = jnp.dot(x_ref[...].astype(BF16), w_ref[...], preferred_element_type=F32).astype(o_ref.dtype)


def _matmul(x, w, out_dtype, tm=512):
    T, K = x.shape
    N = w.shape[1]
    return pl.pallas_call(
        _mm_kernel,
        grid=(T // tm,),
        in_specs=[pl.BlockSpec((tm, K), lambda i: (i, 0)), pl.BlockSpec((K, N), lambda i: (0, 0))],
        out_specs=pl.BlockSpec((tm, N), lambda i: (i, 0)),
        out_shape=jax.ShapeDtypeStruct((T, N), out_dtype),
        compiler_params=_cparams("parallel"),
        name="in_proj",
    )(x, w)


def _head_masks():
    lane = lax.broadcasted_iota(jnp.int32, (1, LANES), 1)
    return lane < HEAD_DIM, lane >= HEAD_DIM


def _select_head(q2, head_lanes):
    return jnp.where(head_lanes, q2 * jnp.asarray(HEAD_DIM ** -0.5, BF16), jnp.zeros((), BF16))


def _softmax_pv(s, v):
    m = jnp.max(s, axis=1, keepdims=True)
    p = jnp.exp(s - m)
    l = jnp.sum(p, axis=1, keepdims=True)
    return jnp.dot(p.astype(BF16), v, preferred_element_type=F32) / l


def _nt_dot(a, b):
    return lax.dot_general(a, b, (((1,), (1,)), ((), ())), preferred_element_type=F32)


def _attn_kernel(q_ref, k_ref, v_ref, w_ref, o_ref, *, S, packed, per_head_w):
    lo, hi = _head_masks()
    for i in range(S // BQ):
        n = (i + 1) * BQ
        rows = slice(i * BQ, (i + 1) * BQ)
        v2 = v_ref[0, :n, :]
        outs = []
        for h in range(2):
            if packed:
                qh = _select_head(q_ref[0, rows, :], lo if h == 0 else hi)
                kh = k_ref[0, :n, :]
            else:
                qh = q_ref[0, rows, h * LANES:(h + 1) * LANES]
                kh = k_ref[0, :n, h * LANES:(h + 1) * LANES]
            s = _nt_dot(qh, kh) + w_ref[h if per_head_w else 0, :, S - n:]
            outs.append(_softmax_pv(s, v2))
        o_ref[0, rows, :] = jnp.where(lo, outs[0], outs[1]).astype(o_ref.dtype)


def _attention(q_arr, k_arr, v_arr, w, *, q_blk, k_blk, v_blk, n_pairs, packed, per_head_w, name):
    B, S, _ = q_arr.shape
    qk_w = LANES if packed else 2 * LANES
    wh = 2 if per_head_w else 1
    return pl.pallas_call(
        functools.partial(_attn_kernel, S=S, packed=packed, per_head_w=per_head_w),
        grid=(n_pairs, B),
        in_specs=[
            pl.BlockSpec((1, S, qk_w), lambda p, b: (b, 0, q_blk + p)),
            pl.BlockSpec((1, S, qk_w), lambda p, b: (b, 0, k_blk + p)),
            pl.BlockSpec((1, S, LANES), lambda p, b: (b, 0, v_blk + p)),
            pl.BlockSpec((wh, BQ, S), (lambda p, b: (p, 0, 0)) if per_head_w else (lambda p, b: (0, 0, 0))),
        ],
        out_specs=pl.BlockSpec((1, S, LANES), lambda p, b: (b, 0, p)),
        out_shape=jax.ShapeDtypeStruct((B, S, n_pairs * LANES), BF16),
        compiler_params=_cparams("parallel", "parallel"),
        name=name,
    )(q_arr, k_arr, v_arr, w)


def _count(mask):
    return jnp.sum(jnp.where(mask, 1.0, 0.0).astype(F32), axis=1, keepdims=True)


def _dsa_kernel(sq_ref, sk_ref, sv_ref, iq_ref, ik_ref, iw_ref, w_ref, o_ref, *, S, n_sel):
    lo, hi = _head_masks()
    idx_bits = (S - 1).bit_length()
    ksel = float(n_sel)
    for i in range(S // BQ):
        n = (i + 1) * BQ
        rows = slice(i * BQ, (i + 1) * BQ)
        ikn = ik_ref[0, :n, :]
        iwb = iw_ref[0, rows, :]
        score = jnp.zeros((BQ, n), F32)
        for hh in range(IDX_HEADS):
            c = hh // 2
            iqc = jnp.where(lo if hh % 2 == 0 else hi, iq_ref[0, rows, c * LANES:(c + 1) * LANES],
                            jnp.zeros((), BF16))
            score = score + iwb[:, hh:hh + 1] * jnp.maximum(_nt_dot(iqc, ikn), 0.0)
        col = lax.broadcasted_iota(jnp.int32, (BQ, n), 1)
        row = lax.broadcasted_iota(jnp.int32, (BQ, n), 0)
        bits = lax.bitcast_convert_type(score, jnp.int32)
        key = bits ^ ((bits >> 31) & jnp.int32(0x7FFFFFFF))
        key = jnp.where(col <= row + i * BQ, key, jnp.int32(INT_MIN))

        tau0 = jnp.where(_count(key >= 0) >= ksel, jnp.int32(0), jnp.int32(INT_MIN))

        def tau_step(it, tau, key=key):
            cand = tau | jnp.left_shift(jnp.int32(1), 30 - it)
            return jnp.where(_count(key >= cand) >= ksel, cand, tau)

        tau = lax.fori_loop(0, 31, tau_step, tau0)
        gt = key > tau
        eq = key == tau
        need = ksel - _count(gt)
        excess = (_count(eq) > need) & (tau > jnp.int32(INT_MIN))
        any_excess = jnp.max(jnp.where(excess, 1, 0).astype(jnp.int32)) > 0

        def idx_step(it, j, eq=eq, col=col, need=need):
            cand = j + jnp.left_shift(jnp.int32(1), idx_bits - 1 - it)
            return jnp.where(_count(eq & (col < cand)) < need, cand, j)

        jmax = lax.fori_loop(0, jnp.where(any_excess, idx_bits, 0), idx_step, jnp.zeros((BQ, 1), jnp.int32))
        jmax = jnp.where(excess, jmax, jnp.int32(S))
        madd = jnp.where(gt | (eq & (col <= jmax)), 0.0, NEG).astype(F32)

        for p in range(N_HEADS_DSA // 2):
            q2 = sq_ref[0, rows, p * LANES:(p + 1) * LANES]
            k2 = sk_ref[0, :n, p * LANES:(p + 1) * LANES]
            v2 = sv_ref[0, :n, p * LANES:(p + 1) * LANES]
            outs = []
            for h in range(2):
                qh = _select_head(q2, lo if h == 0 else hi)
                s = _nt_dot(qh, k2) + w_ref[2 * p + h, :, S - n:] + madd
                outs.append(_softmax_pv(s, v2))
            o_ref[0, rows, p * LANES:(p + 1) * LANES] = jnp.where(lo, outs[0], outs[1]).astype(o_ref.dtype)


def _dsa(ua3, ub3, w):
    B, S, _ = ua3.shape
    n_sel = min(DSA_TOPK, S // 4)
    assert n_sel <= BQ
    hw = N_HEADS_DSA * HEAD_DIM
    return pl.pallas_call(
        functools.partial(_dsa_kernel, S=S, n_sel=n_sel),
        grid=(B,),
        in_specs=[
            pl.BlockSpec((1, S, hw), lambda b: (b, 0, SQ_BLK * LANES // hw)),
            pl.BlockSpec((1, S, hw), lambda b: (b, 0, SK_BLK * LANES // hw)),
            pl.BlockSpec((1, S, hw), lambda b: (b, 0, SV_BLK * LANES // hw)),
            pl.BlockSpec((1, S, IDX_HEADS * IDX_DIM), lambda b: (b, 0, IQ_BLK * LANES // (IDX_HEADS * IDX_DIM))),
            pl.BlockSpec((1, S, LANES), lambda b: (b, 0, IK_BLK)),
            pl.BlockSpec((1, S, LANES), lambda b: (b, 0, (MLA_Q_RANK + MLA_KV_RANK) // LANES)),
            pl.BlockSpec((N_HEADS_DSA, BQ, S), lambda b: (0, 0, 0)),
        ],
        out_specs=pl.BlockSpec((1, S, hw), lambda b: (b, 0, 0)),
        out_shape=jax.ShapeDtypeStruct((B, S, hw), BF16),
        compiler_params=_cparams("parallel"),
        name="dsa_attn",
    )(ua3, ua3, ua3, ua3, ua3, ub3, w)


def _mla_prep_kernel(ub_ref, qn_ref, kvn_ref, wq_ref, wk_ref, wv_ref, c_ref, s1_ref, s2_ref,
                     q_ref, k_ref, v_ref, *, scale):
    ub = ub_ref[...]
    cq = ub[:, :MLA_Q_RANK]
    ckv = ub[:, MLA_Q_RANK:MLA_Q_RANK + MLA_KV_RANK]
    kr_chunk = ub[:, MLA_Q_RANK + MLA_KV_RANK:]

    def rms(t, g):
        return t * lax.rsqrt(jnp.mean(t * t, axis=-1, keepdims=True) + RMS_EPS) * g

    qn = rms(cq, qn_ref[...]).astype(BF16)
    kvn = rms(ckv, kvn_ref[...]).astype(BF16)
    q = jnp.dot(qn, wq_ref[...], preferred_element_type=F32)
    kn = jnp.dot(kvn, wk_ref[...], preferred_element_type=F32)
    v = jnp.dot(kvn, wv_ref[...], preferred_element_type=F32)
    c, s1, s2 = c_ref[...], s1_ref[...], s2_ref[...]

    def rope(t):
        return t * c + pltpu.roll(t, 16, 1) * s1 + pltpu.roll(t, LANES - 16, 1) * s2

    lane = lax.broadcasted_iota(jnp.int32, (1, LANES), 1)
    kr = jnp.where((lane >= MLA_NOPE) & (lane < MLA_NOPE + MLA_ROPE), rope(kr_chunk), 0.0)
    for h in range(N_HEADS_MLA):
        cols = slice(h * LANES, (h + 1) * LANES)
        q_ref[:, cols] = (rope(q[:, cols]) * scale).astype(BF16)
        k_ref[:, cols] = (kn[:, cols] + kr).astype(BF16)
    v_ref[...] = v.astype(BF16)


def _mla_prep(ub, qn, kvn, wq, wk, wv, c, s1, s2, tm=512):
    T = ub.shape[0]
    scale = (MLA_NOPE + MLA_ROPE) ** -0.5
    row = lambda i: (i, 0)
    fix = lambda i: (0, 0)
    hq = N_HEADS_MLA * LANES
    hv = N_HEADS_MLA * MLA_V
    return pl.pallas_call(
        functools.partial(_mla_prep_kernel, scale=scale),
        grid=(T // tm,),
        in_specs=[
            pl.BlockSpec((tm, WB_COLS), row),
            pl.BlockSpec((1, MLA_Q_RANK), fix), pl.BlockSpec((1, MLA_KV_RANK), fix),
            pl.BlockSpec((MLA_Q_RANK, hq), fix), pl.BlockSpec((MLA_KV_RANK, hq), fix),
            pl.BlockSpec((MLA_KV_RANK, hv), fix),
            pl.BlockSpec((tm, LANES), row), pl.BlockSpec((tm, LANES), row), pl.BlockSpec((tm, LANES), row),
        ],
        out_specs=[pl.BlockSpec((tm, hq), row), pl.BlockSpec((tm, hq), row), pl.BlockSpec((tm, hv), row)],
        out_shape=[jax.ShapeDtypeStruct((T, hq), BF16), jax.ShapeDtypeStruct((T, hq), BF16),
                   jax.ShapeDtypeStruct((T, hv), BF16)],
        compiler_params=_cparams("parallel"),
        name="mla_prep",
    )(ub, qn, kvn, wq, wk, wv, c, s1, s2)


def _layer_norm(y, g, b):
    mu = jnp.mean(y, axis=-1, keepdims=True)
    yc = y - mu
    var = jnp.mean(yc * yc, axis=-1, keepdims=True)
    return yc * lax.rsqrt(var + LN_EPS) * g + b


def _outproj_kernel(oa_ref, ob_ref, oc_ref, wa_ref, wb_ref, wc_ref, x_ref, g_ref, b_ref, o_ref, *, alpha):
    acc = jnp.dot(oa_ref[...], wa_ref[...], preferred_element_type=F32)
    acc += jnp.dot(ob_ref[...], wb_ref[...], preferred_element_type=F32)
    acc += jnp.dot(oc_ref[...], wc_ref[...], preferred_element_type=F32)
    o_ref[...] = _layer_norm(alpha * x_ref[...] + acc, g_ref[...], b_ref[...])


def _outproj_ln(oa, ob, oc, wa, wb, wc, x, g, b, alpha, tm=512):
    T, D = x.shape
    row = lambda i: (i, 0)
    fix = lambda i: (0, 0)
    return pl.pallas_call(
        functools.partial(_outproj_kernel, alpha=alpha),
        grid=(T // tm,),
        in_specs=[pl.BlockSpec((tm, oa.shape[1]), row), pl.BlockSpec((tm, ob.shape[1]), row),
                  pl.BlockSpec((tm, oc.shape[1]), row),
                  pl.BlockSpec(wa.shape, fix), pl.BlockSpec(wb.shape, fix), pl.BlockSpec(wc.shape, fix),
                  pl.BlockSpec((tm, D), row), pl.BlockSpec((1, D), fix), pl.BlockSpec((1, D), fix)],
        out_specs=pl.BlockSpec((tm, D), row),
        out_shape=jax.ShapeDtypeStruct((T, D), F32),
        compiler_params=_cparams("parallel"),
        name="out_proj_ln",
    )(oa, ob, oc, wa, wb, wc, x, g, b)


def _router_kernel(x_ref, rw_ref, rb_ref, g_ref):
    logits = jnp.dot(x_ref[...].astype(BF16), rw_ref[...], preferred_element_type=F32)
    scores = 1.0 / (1.0 + jnp.exp(-logits))
    lane = lax.broadcasted_iota(jnp.int32, logits.shape, 1)
    lane_f = lane.astype(F32)
    work = jnp.where(lane < N_EXPERTS, scores + rb_ref[...], NEG)
    sel = lane < 0
    for _ in range(TOP_K):
        m = jnp.max(work, axis=1, keepdims=True)
        first = jnp.min(jnp.where(work == m, lane_f, float(LANES)), axis=1, keepdims=True)
        pick = lane_f == first
        sel = sel | pick
        work = jnp.where(pick, NEG, work)
    gs = jnp.where(sel, scores, 0.0)
    gates = gs / jnp.sum(gs, axis=1, keepdims=True) * ROUTED_SCALE
    g_ref[...] = jnp.where(lane == N_EXPERTS, 1.0, gates)


def _router(x, rw, rb, tm=512):
    T, D = x.shape
    return pl.pallas_call(
        _router_kernel,
        grid=(T // tm,),
        in_specs=[pl.BlockSpec((tm, D), lambda i: (i, 0)), pl.BlockSpec((D, LANES), lambda i: (0, 0)),
                  pl.BlockSpec((1, LANES), lambda i: (0, 0))],
        out_specs=pl.BlockSpec((tm, LANES), lambda i: (i, 0)),
        out_shape=jax.ShapeDtypeStruct((T, LANES), F32),
        compiler_params=_cparams("parallel"),
        name="router",
    )(x, rw, rb)


def _moe_kernel(x_ref, g_ref, w1_ref, w3_ref, w2_ref, lg_ref, lb_ref, o_ref, acc_ref, xb_ref, *, alpha):
    e = pl.program_id(1)

    @pl.when(e == 0)
    def _():
        acc_ref[...] = jnp.zeros_like(acc_ref)
        xb_ref[...] = x_ref[...].astype(BF16)

    xb = xb_ref[...]
    h1 = jnp.dot(xb, w1_ref[0], preferred_element_type=F32)
    h3 = jnp.dot(xb, w3_ref[0], preferred_element_type=F32)
    lane = lax.broadcasted_iota(jnp.int32, g_ref.shape, 1)
    ge = jnp.sum(jnp.where(lane == e, g_ref[...], 0.0), axis=1, keepdims=True)
    h = h1 / (1.0 + jnp.exp(-h1)) * h3 * ge
    acc_ref[...] += jnp.dot(h.astype(BF16), w2_ref[0], preferred_element_type=F32)

    @pl.when(e == pl.num_programs(1) - 1)
    def _():
        o_ref[...] = _layer_norm(alpha * x_ref[...] + acc_ref[...], lg_ref[...], lb_ref[...])


def _moe(x, gates, w1, w3, w2, lg, lb, alpha, tm=1024):
    T, D = x.shape
    tm = min(tm, T)
    n_e, _, ff = w1.shape
    return pl.pallas_call(
        functools.partial(_moe_kernel, alpha=alpha),
        grid=(T // tm, n_e),
        in_specs=[pl.BlockSpec((tm, D), lambda i, e: (i, 0)), pl.BlockSpec((tm, LANES), lambda i, e: (i, 0)),
                  pl.BlockSpec((1, D, ff), lambda i, e: (e, 0, 0)), pl.BlockSpec((1, D, ff), lambda i, e: (e, 0, 0)),
                  pl.BlockSpec((1, ff, D), lambda i, e: (e, 0, 0)),
                  pl.BlockSpec((1, D), lambda i, e: (0, 0)), pl.BlockSpec((1, D), lambda i, e: (0, 0))],
        out_specs=pl.BlockSpec((tm, D), lambda i, e: (i, 0)),
        out_shape=jax.ShapeDtypeStruct((T, D), F32),
        scratch_shapes=[pltpu.VMEM((tm, D), F32), pltpu.VMEM((tm, D), BF16)],
        compiler_params=_cparams("parallel", "arbitrary"),
        name="moe_ffn",
    )(x, gates, w1, w3, w2, lg, lb)


def _t5_bucket(dist):
    max_exact = N_BUCKETS // 2
    d_f = jnp.maximum(dist, 1).astype(F32)
    large = max_exact + (jnp.log(d_f / max_exact) / math.log(MAX_DISTANCE / max_exact)
                         * (N_BUCKETS - max_exact)).astype(jnp.int32)
    large = jnp.minimum(large, N_BUCKETS - 1)
    return jnp.where(dist < max_exact, dist, large)


def _bias_tables(rel_bias, S):
    dist = np.arange(S)
    e = rel_bias[_t5_bucket(jnp.asarray(dist, jnp.int32))].T
    mult = sum(((dist % d == 0) & (dist <= w)).astype(np.float32) for w, d in DILATIONS)
    e_dil = jnp.where(mult > 0, e[:N_HEADS_DIL] + jnp.log(jnp.maximum(mult, 1.0)), NEG)
    e_all = jnp.concatenate([e_dil, e[N_HEADS_DIL:], jnp.zeros((1, S), F32)], 0)
    period = S + BQ
    n_h = e_all.shape[0]
    a = jnp.concatenate([e_all[:, :S - BQ + 1][:, ::-1], jnp.full((n_h, BQ), NEG, F32),
                         e_all[:, S - BQ + 1:][:, ::-1]], 1)
    w = jnp.tile(a, (1, BQ))[:, :BQ * (period - 1)].reshape(n_h, BQ, period - 1)[:, :, :S].astype(F32)
    return w[:N_HEADS_DIL], w[N_HEADS_DIL:N_HEADS_DIL + N_HEADS_DSA], w[-1:]


def _rope_tables(positions):
    inv = ROPE_THETA ** (-jnp.arange(0, MLA_ROPE, 2, dtype=F32) / MLA_ROPE)
    ang = positions.astype(F32).reshape(-1, 1) * inv
    cos, sin = jnp.cos(ang), jnp.sin(ang)
    T, half = cos.shape
    one = jnp.ones((T, MLA_NOPE), F32)
    z = lambda n: jnp.zeros((T, n), F32)
    tail = LANES - MLA_NOPE - MLA_ROPE
    c = jnp.concatenate([one, cos, cos, jnp.ones((T, tail), F32)], 1)
    s1 = jnp.concatenate([z(MLA_NOPE + half), sin, z(tail)], 1)
    s2 = jnp.concatenate([z(MLA_NOPE), -sin, z(half + tail)], 1)
    return c, s1, s2


def _split_w_in(w):
    o = np.cumsum([0, 384, 384, 384, 256, 256, 256, 512, 64, 8, 256, 128, 32])
    ik = w[:, o[7]:o[8]]
    wa = jnp.concatenate([w[:, o[6]:o[7]], w[:, o[3]:o[6]], ik, ik, w[:, :o[3]]], 1)
    z = lambda n: jnp.zeros((w.shape[0], n), w.dtype)
    chunk = jnp.concatenate([w[:, o[8]:o[9]], z(MLA_NOPE - IDX_HEADS), w[:, o[11]:o[12]],
                             z(LANES - MLA_NOPE - MLA_ROPE)], 1)
    wb = jnp.concatenate([w[:, o[9]:o[11]], chunk], 1)
    return wa.astype(BF16), wb.astype(BF16)


def _split_mla_weights(w_uq, w_ukv):
    r = w_uq.shape[0]
    wq = w_uq.reshape(r, N_HEADS_MLA, MLA_NOPE + MLA_ROPE)
    wq = jnp.pad(wq, ((0, 0), (0, 0), (0, LANES - MLA_NOPE - MLA_ROPE))).reshape(r, N_HEADS_MLA * LANES)
    r = w_ukv.shape[0]
    wkv = w_ukv.reshape(r, N_HEADS_MLA, MLA_NOPE + MLA_V)
    wk = jnp.pad(wkv[:, :, :MLA_NOPE], ((0, 0), (0, 0), (0, LANES - MLA_NOPE))).reshape(r, N_HEADS_MLA * LANES)
    wv = wkv[:, :, MLA_NOPE:].reshape(r, N_HEADS_MLA * MLA_V)
    return wq.astype(BF16), wk.astype(BF16), wv.astype(BF16)


def kernel(x, positions, w_in, mla_q_norm, mla_w_uq, mla_kv_norm, mla_w_ukv, w_out, ln1_g, ln1_b, router_w,
           router_bias, exp_w1, exp_w3, exp_w2, sh_w1, sh_w3, sh_w2, ln2_g, ln2_b, rel_bias):
    B, S, D = x.shape
    T = B * S
    depth = w_in.shape[0]
    alpha = (2 * depth) ** 0.25
    w_dil, w_dsa, w_causal = _bias_tables(rel_bias, S)
    rope_c, rope_s1, rope_s2 = _rope_tables(positions)
    na, nb = N_HEADS_DIL * HEAD_DIM, (N_HEADS_DIL + N_HEADS_DSA) * HEAD_DIM
    xf = x.reshape(T, D)
    for l in range(depth):
        wa, wb = _split_w_in(w_in[l])
        ua = _matmul(xf, wa, BF16)
        ub = _matmul(xf, wb, F32)
        ua3 = ua.reshape(B, S, WA_COLS)
        o_a = _attention(ua3, ua3, ua3, w_dil, q_blk=DQ_BLK, k_blk=DK_BLK, v_blk=DV_BLK,
                         n_pairs=N_HEADS_DIL // 2, packed=True, per_head_w=True, name="dilated_attn")
        o_b = _dsa(ua3, ub.reshape(B, S, WB_COLS), w_dsa)
        wq, wk, wv = _split_mla_weights(mla_w_uq[l], mla_w_ukv[l])
        q, k, v = _mla_prep(ub, mla_q_norm[l][None], mla_kv_norm[l][None], wq, wk, wv, rope_c, rope_s1, rope_s2)
        o_c = _attention(q.reshape(B, S, -1), k.reshape(B, S, -1), v.reshape(B, S, -1), w_causal,
                         q_blk=0, k_blk=0, v_blk=0, n_pairs=N_HEADS_MLA // 2, packed=False, per_head_w=False,
                         name="mla_attn")
        wo = w_out[l].astype(BF16)
        x1 = _outproj_ln(o_a.reshape(T, -1), o_b.reshape(T, -1), o_c.reshape(T, -1),
                         wo[:na], wo[na:nb], wo[nb:], xf, ln1_g[l][None], ln1_b[l][None], alpha)
        rw = jnp.pad(router_w[l], ((0, 0), (0, LANES - N_EXPERTS))).astype(BF16)
        rb = jnp.pad(router_bias[l].astype(F32), (0, LANES - N_EXPERTS))[None]
        gates = _router(x1, rw, rb)
        w1 = jnp.concatenate([exp_w1[l], sh_w1[l][None]], 0).astype(BF16)
        w3 = jnp.concatenate([exp_w3[l], sh_w3[l][None]], 0).astype(BF16)
        w2 = jnp.concatenate([exp_w2[l], sh_w2[l][None]], 0).astype(BF16)
        xf = _moe(x1, gates, w1, w3, w2, ln2_g[l][None], ln2_b[l][None], alpha)
    return xf.reshape(B, S, D)
```

```python
import functools
import math

import jax
import jax.numpy as jnp
import numpy as np
from jax import lax
from jax.experimental import pallas as pl
from jax.experimental.pallas import tpu as pltpu

F32 = jnp.float32
BF16 = jnp.bfloat16

D_MODEL = 1024
HEAD_DIM = 64
N_HEADS_DIL = 6
N_HEADS_DSA = 4
N_HEADS_MLA = 6
DILATIONS = ((128, 1), (512, 4), (2048, 16))
IDX_HEADS = 8
IDX_DIM = 64
DSA_TOPK = 256
MLA_Q_RANK = 256
MLA_KV_RANK = 128
MLA_NOPE = 64
MLA_ROPE = 32
MLA_V = 64
ROPE_THETA = 10000.0
N_BUCKETS = 32
MAX_DISTANCE = 2048
N_EXPERTS = 64
TOP_K = 8
EXPERT_FF = 256
ROUTED_SCALE = 2.5
LN_EPS = 1e-5
RMS_EPS = 1e-6

LANES = 128
NEG = -1e30
INT_MIN = -2147483648
VMEM_LIMIT = 56 * 1024 * 1024
BQ = 256

WA_COLS = 3 * N_HEADS_DIL * HEAD_DIM + 3 * N_HEADS_DSA * HEAD_DIM + IDX_HEADS * IDX_DIM + 2 * IDX_DIM
WB_COLS = MLA_Q_RANK + MLA_KV_RANK + LANES
IQ_BLK = 0
SQ_BLK, SK_BLK, SV_BLK = 4, 6, 8
IK_BLK = 10
DQ_BLK, DK_BLK, DV_BLK = 11, 14, 17


def _cparams(*sem):
    return pltpu.CompilerParams(dimension_semantics=sem, vmem_limit_bytes=VMEM_LIMIT)


def _mm_kernel(x_ref, w_ref, o_ref):
    o_ref[...] = jnp.dot(x_ref[...].astype(BF16), w_ref[...], preferred_element_type=F32).astype(o_ref.dtype)


def _matmul(x, w, out_dtype, tm=512):
    T, K = x.shape
    N = w.shape[1]
    return pl.pallas_call(
        _mm_kernel,
        grid=(T // tm,),
        in_specs=[pl.BlockSpec((tm, K), lambda i: (i, 0)), pl.BlockSpec((K, N), lambda i: (0, 0))],
        out_specs=pl.BlockSpec((tm, N), lambda i: (i, 0)),
        out_shape=jax.ShapeDtypeStruct((T, N), out_dtype),
        compiler_params=_cparams("parallel"),
        name="in_proj",
    )(x, w)


def _head_masks():
    lane = lax.broadcasted_iota(jnp.int32, (1, LANES), 1)
    return lane < HEAD_DIM, lane >= HEAD_DIM


def _select_head(q2, head_lanes):
    return jnp.where(head_lanes, q2 * jnp.asarray(HEAD_DIM ** -0.5, BF16), jnp.zeros((), BF16))


def _softmax_pv(s, v):
    m = jnp.max(s, axis=1, keepdims=True)
    p = jnp.exp(s - m)
    l = jnp.sum(p, axis=1, keepdims=True)
    return jnp.dot(p.astype(BF16), v, preferred_element_type=F32) / l


def _nt_dot(a, b):
    return lax.dot_general(a, b, (((1,), (1,)), ((), ())), preferred_element_type=F32)


def _attn_kernel(q_ref, k_ref, v_ref, w_ref, o_ref, *, S, packed, per_head_w):
    lo, hi = _head_masks()
    for i in range(S // BQ):
        n = (i + 1) * BQ
        rows = slice(i * BQ, (i + 1) * BQ)
        v2 = v_ref[0, :n, :]
        outs = []
        for h in range(2):
            if packed:
                qh = _select_head(q_ref[0, rows, :], lo if h == 0 else hi)
                kh = k_ref[0, :n, :]
            else:
                qh = q_ref[0, rows, h * LANES:(h + 1) * LANES]
                kh = k_ref[0, :n, h * LANES:(h + 1) * LANES]
            s = _nt_dot(qh, kh) + w_ref[h if per_head_w else 0, :, S - n:]
            outs.append(_softmax_pv(s, v2))
        o_ref[0, rows, :] = jnp.where(lo, outs[0], outs[1]).astype(o_ref.dtype)


def _attention(q_arr, k_arr, v_arr, w, *, q_blk, k_blk, v_blk, n_pairs, packed, per_head_w, name):
    B, S, _ = q_arr.shape
    qk_w = LANES if packed else 2 * LANES
    wh = 2 if per_head_w else 1
    return pl.pallas_call(
        functools.partial(_attn_kernel, S=S, packed=packed, per_head_w=per_head_w),
        grid=(n_pairs, B),
        in_specs=[
            pl.BlockSpec((1, S, qk_w), lambda p, b: (b, 0, q_blk + p)),
            pl.BlockSpec((1, S, qk_w), lambda p, b: (b, 0, k_blk + p)),
            pl.BlockSpec((1, S, LANES), lambda p, b: (b, 0, v_blk + p)),
            pl.BlockSpec((wh, BQ, S), (lambda p, b: (p, 0, 0)) if per_head_w else (lambda p, b: (0, 0, 0))),
        ],
        out_specs=pl.BlockSpec((1, S, LANES), lambda p, b: (b, 0, p)),
        out_shape=jax.ShapeDtypeStruct((B, S, n_pairs * LANES), BF16),
        compiler_params=_cparams("parallel", "parallel"),
        name=name,
    )(q_arr, k_arr, v_arr, w)


def _count(mask):
    return jnp.sum(jnp.where(mask, 1.0, 0.0).astype(F32), axis=1, keepdims=True)


def _dsa_kernel(sq_ref, sk_ref, sv_ref, iq_ref, ik_ref, iw_ref, w_ref, o_ref, *, S, n_sel):
    lo, hi = _head_masks()
    idx_bits = (S - 1).bit_length()
    ksel = float(n_sel)
    for i in range(S // BQ):
        n = (i + 1) * BQ
        rows = slice(i * BQ, (i + 1) * BQ)
        ikn = ik_ref[0, :n, :]
        iwb = iw_ref[0, rows, :]
        score = jnp.zeros((BQ, n), F32)
        for hh in range(IDX_HEADS):
            c = hh // 2
            iqc = jnp.where(lo if hh % 2 == 0 else hi, iq_ref[0, rows, c * LANES:(c + 1) * LANES],
                            jnp.zeros((), BF16))
            score = score + iwb[:, hh:hh + 1] * jnp.maximum(_nt_dot(iqc, ikn), 0.0)
        col = lax.broadcasted_iota(jnp.int32, (BQ, n), 1)
        row = lax.broadcasted_iota(jnp.int32, (BQ, n), 0)
        bits = lax.bitcast_convert_type(score, jnp.int32)
        key = bits ^ ((bits >> 31) & jnp.int32(0x7FFFFFFF))
        key = jnp.where(col <= row + i * BQ, key, jnp.int32(INT_MIN))

        tau0 = jnp.where(_count(key >= 0) >= ksel, jnp.int32(0), jnp.int32(INT_MIN))

        def tau_step(it, tau, key=key):
            cand = tau | jnp.left_shift(jnp.int32(1), 30 - it)
            return jnp.where(_count(key >= cand) >= ksel, cand, tau)

        tau = lax.fori_loop(0, 31, tau_step, tau0)
        gt = key > tau
        eq = key == tau
        need = ksel - _count(gt)
        excess = (_count(eq) > need) & (tau > jnp.int32(INT_MIN))
        any_excess = jnp.max(jnp.where(excess, 1, 0).astype(jnp.int32)) > 0

        def idx_step(it, j, eq=eq, col=col, need=need):
            cand = j + jnp.left_shift(jnp.int32(1), idx_bits - 1 - it)
            return jnp.where(_count(eq & (col < cand)) < need, cand, j)

        jmax = lax.fori_loop(0, jnp.where(any_excess, idx_bits, 0), idx_step, jnp.zeros((BQ, 1), jnp.int32))
        jmax = jnp.where(excess, jmax, jnp.int32(S))
        madd = jnp.where(gt | (eq & (col <= jmax)), 0.0, NEG).astype(F32)

        for p in range(N_HEADS_DSA // 2):
            q2 = sq_ref[0, rows, p * LANES:(p + 1) * LANES]
            k2 = sk_ref[0, :n, p * LANES:(p + 1) * LANES]
            v2 = sv_ref[0, :n, p * LANES:(p + 1) * LANES]
            outs = []
            for h in range(2):
                qh = _select_head(q2, lo if h == 0 else hi)
                s = _nt_dot(qh, k2) + w_ref[2 * p + h, :, S - n:] + madd
                outs.append(_softmax_pv(s, v2))
            o_ref[0, rows, p * LANES:(p + 1) * LANES] = jnp.where(lo, outs[0], outs[1]).astype(o_ref.dtype)


def _dsa(ua3, ub3, w):
    B, S, _ = ua3.shape
    n_sel = min(DSA_TOPK, S // 4)
    assert n_sel <= BQ
    hw = N_HEADS_DSA * HEAD_DIM
    return pl.pallas_call(
        functools.partial(_dsa_kernel, S=S, n_sel=n_sel),
        grid=(B,),
        in_specs=[
            pl.BlockSpec((1, S, hw), lambda b: (b, 0, SQ_BLK * LANES // hw)),
            pl.BlockSpec((1, S, hw), lambda b: (b, 0, SK_BLK * LANES // hw)),
            pl.BlockSpec((1, S, hw), lambda b: (b, 0, SV_BLK * LANES // hw)),
            pl.BlockSpec((1, S, IDX_HEADS * IDX_DIM), lambda b: (b, 0, IQ_BLK * LANES // (IDX_HEADS * IDX_DIM))),
            pl.BlockSpec((1, S, LANES), lambda b: (b, 0, IK_BLK)),
            pl.BlockSpec((1, S, LANES), lambda b: (b, 0, (MLA_Q_RANK + MLA_KV_RANK) // LANES)),
            pl.BlockSpec((N_HEADS_DSA, BQ, S), lambda b: (0, 0, 0)),
        ],
        out_specs=pl.BlockSpec((1, S, hw), lambda b: (b, 0, 0)),
        out_shape=jax.ShapeDtypeStruct((B, S, hw), BF16),
        compiler_params=_cparams("parallel"),
        name="dsa_attn",
    )(ua3, ua3, ua3, ua3, ua3, ub3, w)


def _mla_prep_kernel(ub_ref, qn_ref, kvn_ref, wq_ref, wk_ref, wv_ref, c_ref, s1_ref, s2_ref,
                     q_ref, k_ref, v_ref, *, scale):
    ub = ub_ref[...]
    cq = ub[:, :MLA_Q_RANK]
    ckv = ub[:, MLA_Q_RANK:MLA_Q_RANK + MLA_KV_RANK]
    kr_chunk = ub[:, MLA_Q_RANK + MLA_KV_RANK:]

    def rms(t, g):
        return t * lax.rsqrt(jnp.mean(t * t, axis=-1, keepdims=True) + RMS_EPS) * g

    qn = rms(cq, qn_ref[...]).astype(BF16)
    kvn = rms(ckv, kvn_ref[...]).astype(BF16)
    q = jnp.dot(qn, wq_ref[...], preferred_element_type=F32)
    kn = jnp.dot(kvn, wk_ref[...], preferred_element_type=F32)
    v = jnp.dot(kvn, wv_ref[...], preferred_element_type=F32)
    c, s1, s2 = c_ref[...], s1_ref[...], s2_ref[...]

    def rope(t):
        return t * c + pltpu.roll(t, 16, 1) * s1 + pltpu.roll(t, LANES - 16, 1) * s2

    lane = lax.broadcasted_iota(jnp.int32, (1, LANES), 1)
    kr = jnp.where((lane >= MLA_NOPE) & (lane < MLA_NOPE + MLA_ROPE), rope(kr_chunk), 0.0)
    for h in range(N_HEADS_MLA):
        cols = slice(h * LANES, (h + 1) * LANES)
        q_ref[:, cols] = (rope(q[:, cols]) * scale).astype(BF16)
        k_ref[:, cols] = (kn[:, cols] + kr).astype(BF16)
    v_ref[...] = v.astype(BF16)


def _mla_prep(ub, qn, kvn, wq, wk, wv, c, s1, s2, tm=512):
    T = ub.shape[0]
    scale = (MLA_NOPE + MLA_ROPE) ** -0.5
    row = lambda i: (i, 0)
    fix = lambda i: (0, 0)
    hq = N_HEADS_MLA * LANES
    hv = N_HEADS_MLA * MLA_V
    return pl.pallas_call(
        functools.partial(_mla_prep_kernel, scale=scale),
        grid=(T // tm,),
        in_specs=[
            pl.BlockSpec((tm, WB_COLS), row),
            pl.BlockSpec((1, MLA_Q_RANK), fix), pl.BlockSpec((1, MLA_KV_RANK), fix),
            pl.BlockSpec((MLA_Q_RANK, hq), fix), pl.BlockSpec((MLA_KV_RANK, hq), fix),
            pl.BlockSpec((MLA_KV_RANK, hv), fix),
            pl.BlockSpec((tm, LANES), row), pl.BlockSpec((tm, LANES), row), pl.BlockSpec((tm, LANES), row),
        ],
        out_specs=[pl.BlockSpec((tm, hq), row), pl.BlockSpec((tm, hq), row), pl.BlockSpec((tm, hv), row)],
        out_shape=[jax.ShapeDtypeStruct((T, hq), BF16), jax.ShapeDtypeStruct((T, hq), BF16),
                   jax.ShapeDtypeStruct((T, hv), BF16)],
        compiler_params=_cparams("parallel"),
        name="mla_prep",
    )(ub, qn, kvn, wq, wk, wv, c, s1, s2)


def _layer_norm(y, g, b):
    mu = jnp.mean(y, axis=-1, keepdims=True)
    yc = y - mu
    var = jnp.mean(yc * yc, axis=-1, keepdims=True)
    return yc * lax.rsqrt(var + LN_EPS) * g + b


def _outproj_kernel(oa_ref, ob_ref, oc_ref, wa_ref, wb_ref, wc_ref, x_ref, g_ref, b_ref, o_ref, *, alpha):
    acc = jnp.dot(oa_ref[...], wa_ref[...], preferred_element_type=F32)
    acc += jnp.dot(ob_ref[...], wb_ref[...], preferred_element_type=F32)
    acc += jnp.dot(oc_ref[...], wc_ref[...], preferred_element_type=F32)
    o_ref[...] = _layer_norm(alpha * x_ref[...] + acc, g_ref[...], b_ref[...])


def _outproj_ln(oa, ob, oc, wa, wb, wc, x, g, b, alpha, tm=512):
    T, D = x.shape
    row = lambda i: (i, 0)
    fix = lambda i: (0, 0)
    return pl.pallas_call(
        functools.partial(_outproj_kernel, alpha=alpha),
        grid=(T // tm,),
        in_specs=[pl.BlockSpec((tm, oa.shape[1]), row), pl.BlockSpec((tm, ob.shape[1]), row),
                  pl.BlockSpec((tm, oc.shape[1]), row),
                  pl.BlockSpec(wa.shape, fix), pl.BlockSpec(wb.shape, fix), pl.BlockSpec(wc.shape, fix),
                  pl.BlockSpec((tm, D), row), pl.BlockSpec((1, D), fix), pl.BlockSpec((1, D), fix)],
        out_specs=pl.BlockSpec((tm, D), row),
        out_shape=jax.ShapeDtypeStruct((T, D), F32),
        compiler_params=_cparams("parallel"),
        name="out_proj_ln",
    )(oa, ob, oc, wa, wb, wc, x, g, b)


MOE_TT = 256
MOE_CH = 8
MOE_BM = 256
MOE_SLOTS = -(-(TOP_K * MOE_TT + N_EXPERTS * (MOE_CH - 1)) // MOE_BM) * MOE_BM
HALF = D_MODEL // 2
HI_MASK = -65536


def _pack_bf16_pairs(t):
    lo = lax.bitcast_convert_type(t[:, :HALF], jnp.int32)
    hi = lax.bitcast_convert_type(t[:, HALF:], jnp.int32)
    return (hi & jnp.int32(HI_MASK)) | lax.shift_right_logical(lo, jnp.int32(16))


def _unpack_bf16_pairs(w):
    lo = lax.bitcast_convert_type(lax.shift_left(w, jnp.int32(16)), F32)
    hi = lax.bitcast_convert_type(w & jnp.int32(HI_MASK), F32)
    return jnp.concatenate([lo, hi], axis=1).astype(BF16)


def _route_kernel(x_ref, rw_ref, rb_ref, slot_ref, gate_ref, slott_ref, meta_ref):
    tt = x_ref.shape[0]
    logits = jnp.dot(x_ref[...].astype(BF16), rw_ref[...], preferred_element_type=F32)
    scores = 1.0 / (1.0 + jnp.exp(-logits))
    lane = lax.broadcasted_iota(jnp.int32, logits.shape, 1)
    lane_f = lane.astype(F32)
    work = jnp.where(lane < N_EXPERTS, scores + rb_ref[...], NEG)
    sel = lane < 0
    firsts = []
    for _ in range(TOP_K):
        m = jnp.max(work, axis=1, keepdims=True)
        first = jnp.min(jnp.where(work == m, lane_f, float(LANES)), axis=1, keepdims=True)
        pick = lane_f == first
        sel = sel | pick
        work = jnp.where(pick, NEG, work)
        firsts.append(first)
    gs = jnp.where(sel, scores, 0.0)
    gates = gs / jnp.sum(gs, axis=1, keepdims=True) * ROUTED_SCALE

    sel_f = jnp.where(sel, 1.0, 0.0).astype(F32)
    r_i = lax.broadcasted_iota(jnp.int32, (tt, tt), 0)
    c_i = lax.broadcasted_iota(jnp.int32, (tt, tt), 1)
    below = jnp.where(r_i > c_i, 1.0, 0.0).astype(BF16)
    rank = jnp.dot(below, sel_f.astype(BF16), preferred_element_type=F32)
    cnt = jnp.sum(sel_f, axis=0, keepdims=True)
    cnt_pad = jnp.floor((cnt + (MOE_CH - 1)) * (1.0 / MOE_CH)) * MOE_CH
    e_r = lax.broadcasted_iota(jnp.int32, (LANES, LANES), 0)
    e_c = lax.broadcasted_iota(jnp.int32, (LANES, LANES), 1)
    before = jnp.where(e_r < e_c, 1.0, 0.0).astype(BF16)
    base = jnp.dot(jnp.broadcast_to(cnt_pad, (8, LANES)).astype(BF16), before,
                   preferred_element_type=F32)[0:1]
    slot = base + rank

    slot_tk = jnp.full(logits.shape, -1.0, F32)
    gate_tk = jnp.zeros(logits.shape, F32)
    for k, first in enumerate(firsts):
        pick = lane_f == first
        slot_k = jnp.sum(jnp.where(pick, slot, 0.0), axis=1, keepdims=True)
        gate_k = jnp.sum(jnp.where(pick, gates, 0.0), axis=1, keepdims=True)
        slot_tk = jnp.where(lane == k, slot_k, slot_tk)
        gate_tk = jnp.where(lane == k, gate_k, gate_tk)
    slot_ref[...] = slot_tk.astype(jnp.int32)
    gate_ref[...] = gate_tk
    slott_ref[0] = slot_tk.T[:TOP_K].astype(jnp.int32)
    row = lax.broadcasted_iota(jnp.int32, (8, LANES), 0)
    meta = jnp.where(row == 0, jnp.broadcast_to(cnt_pad, (8, LANES)), jnp.broadcast_to(base, (8, LANES)))
    meta_ref[0] = meta.astype(jnp.int32)


def _route(x, rw, rb):
    T, D = x.shape
    nt = T // MOE_TT
    return pl.pallas_call(
        _route_kernel,
        grid=(nt,),
        in_specs=[pl.BlockSpec((MOE_TT, D), lambda i: (i, 0)), pl.BlockSpec((D, LANES), lambda i: (0, 0)),
                  pl.BlockSpec((1, LANES), lambda i: (0, 0))],
        out_specs=[pl.BlockSpec((MOE_TT, LANES), lambda i: (i, 0)), pl.BlockSpec((MOE_TT, LANES), lambda i: (i, 0)),
                   pl.BlockSpec((1, TOP_K, MOE_TT), lambda i: (i, 0, 0)), pl.BlockSpec((1, 8, LANES), lambda i: (i, 0, 0))],
        out_shape=[jax.ShapeDtypeStruct((T, LANES), jnp.int32), jax.ShapeDtypeStruct((T, LANES), F32),
                   jax.ShapeDtypeStruct((nt, TOP_K, MOE_TT), jnp.int32), jax.ShapeDtypeStruct((nt, 8, LANES), jnp.int32)],
        compiler_params=_cparams("parallel"),
        name="moe_route",
    )(x, rw, rb)


def _chunk_copy(src_ref, dst_ref, sem):
    return pltpu.make_async_copy(src_ref, dst_ref, sem)


def _dispatch_kernel(base_ref, gstart_ref, nchunk_ref, tchunks_ref, tail_start_ref, tail_n_ref,
                     x_ref, slott_ref, xs_ref, loc_ref, zero_ref, sem, tail_sem):
    i = pl.program_id(0)
    nt = pl.num_programs(0)
    cur = i % 2
    s_iota = lax.broadcasted_iota(jnp.int32, (MOE_SLOTS, MOE_TT), 0)
    hit = s_iota == slott_ref[0, 0:1, :]
    for k in range(1, TOP_K):
        hit = hit | (s_iota == slott_ref[0, k:k + 1, :])
    onehot = jnp.where(hit, 1.0, 0.0).astype(BF16)
    rows = jnp.dot(onehot, x_ref[...].astype(BF16), preferred_element_type=F32)
    loc_ref[cur] = _pack_bf16_pairs(rows)

    def wait_chunks(n):
        def body(_, c):
            _chunk_copy(loc_ref.at[0, pl.ds(0, MOE_CH), :], xs_ref.at[pl.ds(0, MOE_CH), :], sem).wait()
            return c
        lax.fori_loop(0, n, body, 0)

    @pl.when(i > 0)
    def _():
        wait_chunks(tchunks_ref[i - 1])

    def per_expert(e, c):
        b = base_ref[i * N_EXPERTS + e]
        g = gstart_ref[i * N_EXPERTS + e]

        def per_chunk(j, c2):
            src = loc_ref.at[cur, pl.ds(pl.multiple_of(b + j * MOE_CH, MOE_CH), MOE_CH), :]
            dst = xs_ref.at[pl.ds(pl.multiple_of(g + j * MOE_CH, MOE_CH), MOE_CH), :]
            _chunk_copy(src, dst, sem).start()
            return c2
        return lax.fori_loop(0, nchunk_ref[i * N_EXPERTS + e], per_chunk, c)

    lax.fori_loop(0, N_EXPERTS, per_expert, 0)

    @pl.when(i == nt - 1)
    def _():
        zero_ref[...] = jnp.zeros_like(zero_ref)

        def tail_expert(e, c):
            g = tail_start_ref[e]

            def tail_chunk(j, c2):
                dst = xs_ref.at[pl.ds(pl.multiple_of(g + j * MOE_CH, MOE_CH), MOE_CH), :]
                _chunk_copy(zero_ref, dst, tail_sem).start()
                return c2
            return lax.fori_loop(0, tail_n_ref[e], tail_chunk, c)

        lax.fori_loop(0, N_EXPERTS + 1, tail_expert, 0)

        def tail_wait(e, c):
            def body(_, c2):
                _chunk_copy(zero_ref, xs_ref.at[pl.ds(0, MOE_CH), :], tail_sem).wait()
                return c2
            return lax.fori_loop(0, tail_n_ref[e], body, c)

        lax.fori_loop(0, N_EXPERTS + 1, tail_wait, 0)
        wait_chunks(tchunks_ref[i])


def _dispatch(x, slot_kt, tables, n_rows):
    T, D = x.shape
    nt = T // MOE_TT
    base, gstart, nchunk, tchunks, tail_start, tail_n = tables
    return pl.pallas_call(
        _dispatch_kernel,
        grid_spec=pltpu.PrefetchScalarGridSpec(
            num_scalar_prefetch=6,
            grid=(nt,),
            in_specs=[pl.BlockSpec((MOE_TT, D), lambda i, *_: (i, 0)),
                      pl.BlockSpec((1, TOP_K, MOE_TT), lambda i, *_: (i, 0, 0))],
            out_specs=pl.BlockSpec(memory_space=pl.ANY),
            scratch_shapes=[pltpu.VMEM((2, MOE_SLOTS, HALF), jnp.int32), pltpu.VMEM((MOE_CH, HALF), jnp.int32),
                            pltpu.SemaphoreType.DMA(()), pltpu.SemaphoreType.DMA(())],
        ),
        out_shape=jax.ShapeDtypeStruct((n_rows, HALF), jnp.int32),
        compiler_params=_cparams("arbitrary"),
        name="moe_dispatch",
    )(base, gstart, nchunk, tchunks, tail_start, tail_n, x, slot_kt)


def _expert_kernel(be_ref, nu_ref, xs_ref, w1_ref, w3_ref, w2_ref, ys_ref, w1b_ref, w3b_ref, w2b_ref):
    i = pl.program_id(0)

    @pl.when((i == 0) | (be_ref[i] != be_ref[jnp.maximum(i - 1, 0)]))
    def _():
        w1b_ref[...] = w1_ref[0].astype(BF16)
        w3b_ref[...] = w3_ref[0].astype(BF16)
        w2b_ref[...] = w2_ref[0].astype(BF16)

    @pl.when(i < nu_ref[0])
    def _():
        xb = _unpack_bf16_pairs(xs_ref[...])
        h1 = jnp.dot(xb, w1b_ref[...], preferred_element_type=F32)
        h3 = jnp.dot(xb, w3b_ref[...], preferred_element_type=F32)
        h = h1 / (1.0 + jnp.exp(-h1)) * h3
        y = jnp.dot(h.astype(BF16), w2b_ref[...], preferred_element_type=F32)
        ys_ref[...] = _pack_bf16_pairs(y.astype(BF16).astype(F32))

    @pl.when(i >= nu_ref[0])
    def _():
        ys_ref[...] = jnp.zeros_like(ys_ref)


def _experts(xs, block_expert, n_used, w1, w3, w2):
    n_rows = xs.shape[0]
    nb = n_rows // MOE_BM
    _, D, ff = w1.shape
    blk = lambda i, be, nu: (jnp.minimum(i, nu[0] - 1), 0)
    return pl.pallas_call(
        _expert_kernel,
        grid_spec=pltpu.PrefetchScalarGridSpec(
            num_scalar_prefetch=2,
            grid=(nb,),
            in_specs=[pl.BlockSpec((MOE_BM, HALF), blk),
                      pl.BlockSpec((1, D, ff), lambda i, be, nu: (be[i], 0, 0)),
                      pl.BlockSpec((1, D, ff), lambda i, be, nu: (be[i], 0, 0)),
                      pl.BlockSpec((1, ff, D), lambda i, be, nu: (be[i], 0, 0))],
            out_specs=pl.BlockSpec((MOE_BM, HALF), lambda i, be, nu: (i, 0)),
            scratch_shapes=[pltpu.VMEM((D, ff), BF16), pltpu.VMEM((D, ff), BF16), pltpu.VMEM((ff, D), BF16)],
        ),
        out_shape=jax.ShapeDtypeStruct((n_rows, HALF), jnp.int32),
        compiler_params=_cparams("arbitrary"),
        name="moe_experts",
    )(block_expert, n_used, xs, w1, w3, w2)


def _combine_kernel(base_ref, gstart_ref, nchunk_ref, tchunks_ref, slot_ref, gate_ref, ys_ref, o_ref, loc_ref, sem):
    i = pl.program_id(0)
    nt = pl.num_programs(0)
    cur = i % 2

    def fetch(t, buf):
        def per_expert(e, c):
            b = base_ref[t * N_EXPERTS + e]
            g = gstart_ref[t * N_EXPERTS + e]

            def per_chunk(j, c2):
                src = ys_ref.at[pl.ds(pl.multiple_of(g + j * MOE_CH, MOE_CH), MOE_CH), :]
                dst = loc_ref.at[buf, pl.ds(pl.multiple_of(b + j * MOE_CH, MOE_CH), MOE_CH), :]
                _chunk_copy(src, dst, sem.at[buf]).start()
                return c2
            return lax.fori_loop(0, nchunk_ref[t * N_EXPERTS + e], per_chunk, c)
        lax.fori_loop(0, N_EXPERTS, per_expert, 0)

    @pl.when(i == 0)
    def _():
        loc_ref[...] = jnp.zeros_like(loc_ref)
        fetch(0, 0)

    @pl.when(i + 1 < nt)
    def _():
        fetch(i + 1, 1 - cur)

    def wait_body(_, c):
        _chunk_copy(ys_ref.at[pl.ds(0, MOE_CH), :], loc_ref.at[cur, pl.ds(0, MOE_CH), :], sem.at[cur]).wait()
        return c
    lax.fori_loop(0, tchunks_ref[i], wait_body, 0)

    s_iota = lax.broadcasted_iota(jnp.int32, (MOE_TT, MOE_SLOTS), 1)
    slots = slot_ref[...]
    gates = gate_ref[...]
    wgt = jnp.zeros((MOE_TT, MOE_SLOTS), F32)
    for k in range(TOP_K):
        wgt = jnp.where(s_iota == slots[:, k:k + 1], gates[:, k:k + 1], wgt)
    w_hi = wgt.astype(BF16)
    w_lo = (wgt - w_hi.astype(F32)).astype(BF16)
    ys = _unpack_bf16_pairs(loc_ref[cur])
    o_ref[...] = (jnp.dot(w_hi, ys, preferred_element_type=F32) + jnp.dot(w_lo, ys, preferred_element_type=F32))


def _combine(ys, slot_tk, gate_tk, tables):
    T = slot_tk.shape[0]
    nt = T // MOE_TT
    base, gstart, nchunk, tchunks = tables
    return pl.pallas_call(
        _combine_kernel,
        grid_spec=pltpu.PrefetchScalarGridSpec(
            num_scalar_prefetch=4,
            grid=(nt,),
            in_specs=[pl.BlockSpec((MOE_TT, LANES), lambda i, *_: (i, 0)),
                      pl.BlockSpec((MOE_TT, LANES), lambda i, *_: (i, 0)),
                      pl.BlockSpec(memory_space=pl.ANY)],
            out_specs=pl.BlockSpec((MOE_TT, D_MODEL), lambda i, *_: (i, 0)),
            scratch_shapes=[pltpu.VMEM((2, MOE_SLOTS, HALF), jnp.int32), pltpu.SemaphoreType.DMA((2,))],
        ),
        out_shape=jax.ShapeDtypeStruct((T, D_MODEL), F32),
        compiler_params=_cparams("arbitrary"),
        name="moe_combine",
    )(base, gstart, nchunk, tchunks, slot_tk, gate_tk, ys)


def _shared_ln_kernel(x_ref, r_ref, w1_ref, w3_ref, w2_ref, g_ref, b_ref, o_ref, *, alpha):
    x = x_ref[...]
    xb = x.astype(BF16)
    h1 = jnp.dot(xb, w1_ref[...], preferred_element_type=F32)
    h3 = jnp.dot(xb, w3_ref[...], preferred_element_type=F32)
    h = h1 / (1.0 + jnp.exp(-h1)) * h3
    y = jnp.dot(h.astype(BF16), w2_ref[...], preferred_element_type=F32)
    o_ref[...] = _layer_norm(alpha * x + (r_ref[...] + y), g_ref[...], b_ref[...])


def _shared_ln(x, routed, w1, w3, w2, g, b, alpha, tm=512):
    T, D = x.shape
    row = lambda i: (i, 0)
    fix = lambda i: (0, 0)
    return pl.pallas_call(
        functools.partial(_shared_ln_kernel, alpha=alpha),
        grid=(T // tm,),
        in_specs=[pl.BlockSpec((tm, D), row), pl.BlockSpec((tm, D), row), pl.BlockSpec(w1.shape, fix),
                  pl.BlockSpec(w3.shape, fix), pl.BlockSpec(w2.shape, fix), pl.BlockSpec((1, D), fix),
                  pl.BlockSpec((1, D), fix)],
        out_specs=pl.BlockSpec((tm, D), row),
        out_shape=jax.ShapeDtypeStruct((T, D), F32),
        compiler_params=_cparams("parallel"),
        name="shared_ffn_ln",
    )(x, routed, w1, w3, w2, g, b)


def _dispatch_tables(meta, n_blocks):
    cnt = meta[:, 0, :N_EXPERTS]
    base = meta[:, 1, :N_EXPERTS]
    tot = jnp.sum(cnt, axis=0)
    nblk = (tot + MOE_BM - 1) // MOE_BM
    bend = jnp.cumsum(nblk)
    pstart = (bend - nblk) * MOE_BM
    gstart = pstart[None, :] + jnp.cumsum(cnt, axis=0) - cnt
    nchunk = cnt // MOE_CH
    n_used = bend[-1:]
    be = jnp.minimum(jnp.searchsorted(bend, jnp.arange(n_blocks, dtype=jnp.int32), side="right"), N_EXPERTS - 1)
    be = jnp.where(jnp.arange(n_blocks) < n_used[0], be, be[jnp.maximum(n_used[0] - 1, 0)])
    i32 = lambda a: a.astype(jnp.int32)
    common = (i32(base.reshape(-1)), i32(gstart.reshape(-1)), i32(nchunk.reshape(-1)), i32(jnp.sum(nchunk, axis=1)))
    tails = (i32(jnp.concatenate([pstart + tot, n_used * MOE_BM])),
             i32(jnp.concatenate([nblk * MOE_BM - tot, (n_blocks - n_used) * MOE_BM]) // MOE_CH))
    return common, tails, i32(be), i32(n_used)


def _moe_block(x1, rw, rb, w1, w3, w2, sw1, sw3, sw2, lg, lb, alpha):
    T = x1.shape[0]
    nt = T // MOE_TT
    n_blocks = -(-(TOP_K * T + nt * N_EXPERTS * (MOE_CH - 1)) // MOE_BM) + N_EXPERTS
    slot_tk, gate_tk, slot_kt, meta = _route(x1, rw, rb)
    common, tails, be, n_used = _dispatch_tables(meta, n_blocks)
    xs = _dispatch(x1, slot_kt, common + tails, n_blocks * MOE_BM)
    ys = _experts(xs, be, n_used, w1, w3, w2)
    routed = _combine(ys, slot_tk, gate_tk, common)
    return _shared_ln(x1, routed, sw1, sw3, sw2, lg, lb, alpha)


def _t5_bucket(dist):
    max_exact = N_BUCKETS // 2
    d_f = jnp.maximum(dist, 1).astype(F32)
    large = max_exact + (jnp.log(d_f / max_exact) / math.log(MAX_DISTANCE / max_exact)
                         * (N_BUCKETS - max_exact)).astype(jnp.int32)
    large = jnp.minimum(large, N_BUCKETS - 1)
    return jnp.where(dist < max_exact, dist, large)


def _bias_tables(rel_bias, S):
    dist = np.arange(S)
    e = rel_bias[_t5_bucket(jnp.asarray(dist, jnp.int32))].T
    mult = sum(((dist % d == 0) & (dist <= w)).astype(np.float32) for w, d in DILATIONS)
    e_dil = jnp.where(mult > 0, e[:N_HEADS_DIL] + jnp.log(jnp.maximum(mult, 1.0)), NEG)
    e_all = jnp.concatenate([e_dil, e[N_HEADS_DIL:], jnp.zeros((1, S), F32)], 0)
    period = S + BQ
    n_h = e_all.shape[0]
    a = jnp.concatenate([e_all[:, :S - BQ + 1][:, ::-1], jnp.full((n_h, BQ), NEG, F32),
                         e_all[:, S - BQ + 1:][:, ::-1]], 1)
    w = jnp.tile(a, (1, BQ))[:, :BQ * (period - 1)].reshape(n_h, BQ, period - 1)[:, :, :S].astype(F32)
    return w[:N_HEADS_DIL], w[N_HEADS_DIL:N_HEADS_DIL + N_HEADS_DSA], w[-1:]


def _rope_tables(positions):
    inv = ROPE_THETA ** (-jnp.arange(0, MLA_ROPE, 2, dtype=F32) / MLA_ROPE)
    ang = positions.astype(F32).reshape(-1, 1) * inv
    cos, sin = jnp.cos(ang), jnp.sin(ang)
    T, half = cos.shape
    one = jnp.ones((T, MLA_NOPE), F32)
    z = lambda n: jnp.zeros((T, n), F32)
    tail = LANES - MLA_NOPE - MLA_ROPE
    c = jnp.concatenate([one, cos, cos, jnp.ones((T, tail), F32)], 1)
    s1 = jnp.concatenate([z(MLA_NOPE + half), sin, z(tail)], 1)
    s2 = jnp.concatenate([z(MLA_NOPE), -sin, z(half + tail)], 1)
    return c, s1, s2


def _split_w_in(w):
    o = np.cumsum([0, 384, 384, 384, 256, 256, 256, 512, 64, 8, 256, 128, 32])
    ik = w[:, o[7]:o[8]]
    wa = jnp.concatenate([w[:, o[6]:o[7]], w[:, o[3]:o[6]], ik, ik, w[:, :o[3]]], 1)
    z = lambda n: jnp.zeros((w.shape[0], n), w.dtype)
    chunk = jnp.concatenate([w[:, o[8]:o[9]], z(MLA_NOPE - IDX_HEADS), w[:, o[11]:o[12]],
                             z(LANES - MLA_NOPE - MLA_ROPE)], 1)
    wb = jnp.concatenate([w[:, o[9]:o[11]], chunk], 1)
    return wa.astype(BF16), wb.astype(BF16)


def _split_mla_weights(w_uq, w_ukv):
    r = w_uq.shape[0]
    wq = w_uq.reshape(r, N_HEADS_MLA, MLA_NOPE + MLA_ROPE)
    wq = jnp.pad(wq, ((0, 0), (0, 0), (0, LANES - MLA_NOPE - MLA_ROPE))).reshape(r, N_HEADS_MLA * LANES)
    r = w_ukv.shape[0]
    wkv = w_ukv.reshape(r, N_HEADS_MLA, MLA_NOPE + MLA_V)
    wk = jnp.pad(wkv[:, :, :MLA_NOPE], ((0, 0), (0, 0), (0, LANES - MLA_NOPE))).reshape(r, N_HEADS_MLA * LANES)
    wv = wkv[:, :, MLA_NOPE:].reshape(r, N_HEADS_MLA * MLA_V)
    return wq.astype(BF16), wk.astype(BF16), wv.astype(BF16)


def kernel(x, positions, w_in, mla_q_norm, mla_w_uq, mla_kv_norm, mla_w_ukv, w_out, ln1_g, ln1_b, router_w,
           router_bias, exp_w1, exp_w3, exp_w2, sh_w1, sh_w3, sh_w2, ln2_g, ln2_b, rel_bias):
    B, S, D = x.shape
    T = B * S
    depth = w_in.shape[0]
    alpha = (2 * depth) ** 0.25
    w_dil, w_dsa, w_causal = _bias_tables(rel_bias, S)
    rope_c, rope_s1, rope_s2 = _rope_tables(positions)
    na, nb = N_HEADS_DIL * HEAD_DIM, (N_HEADS_DIL + N_HEADS_DSA) * HEAD_DIM
    xf = x.reshape(T, D)
    for l in range(depth):
        wa, wb = _split_w_in(w_in[l])
        ua = _matmul(xf, wa, BF16)
        ub = _matmul(xf, wb, F32)
        ua3 = ua.reshape(B, S, WA_COLS)
        o_a = _attention(ua3, ua3, ua3, w_dil, q_blk=DQ_BLK, k_blk=DK_BLK, v_blk=DV_BLK,
                         n_pairs=N_HEADS_DIL // 2, packed=True, per_head_w=True, name="dilated_attn")
        o_b = _dsa(ua3, ub.reshape(B, S, WB_COLS), w_dsa)
        wq, wk, wv = _split_mla_weights(mla_w_uq[l], mla_w_ukv[l])
        q, k, v = _mla_prep(ub, mla_q_norm[l][None], mla_kv_norm[l][None], wq, wk, wv, rope_c, rope_s1, rope_s2)
        o_c = _attention(q.reshape(B, S, -1), k.reshape(B, S, -1), v.reshape(B, S, -1), w_causal,
                         q_blk=0, k_blk=0, v_blk=0, n_pairs=N_HEADS_MLA // 2, packed=False, per_head_w=False,
                         name="mla_attn")
        wo = w_out[l].astype(BF16)
        x1 = _outproj_ln(o_a.reshape(T, -1), o_b.reshape(T, -1), o_c.reshape(T, -1),
                         wo[:na], wo[na:nb], wo[nb:], xf, ln1_g[l][None], ln1_b[l][None], alpha)
        rw = jnp.pad(router_w[l], ((0, 0), (0, LANES - N_EXPERTS))).astype(BF16)
        rb = jnp.pad(router_bias[l].astype(F32), (0, LANES - N_EXPERTS))[None]
        xf = _moe_block(x1, rw, rb, exp_w1[l], exp_w3[l], exp_w2[l], sh_w1[l].astype(BF16), sh_w3[l].astype(BF16),
                        sh_w2[l].astype(BF16), ln2_g[l][None], ln2_b[l][None], alpha)
    return xf.reshape(B, S, D)
```

```python
import functools
import math

import jax
import jax.numpy as jnp
import numpy as np
from jax import lax
from jax.experimental import pallas as pl
from jax.experimental.pallas import tpu as pltpu

F32 = jnp.float32
BF16 = jnp.bfloat16

D_MODEL = 1024
HEAD_DIM = 64
N_HEADS_DIL = 6
N_HEADS_DSA = 4
N_HEADS_MLA = 6
DILATIONS = ((128, 1), (512, 4), (2048, 16))
IDX_HEADS = 8
IDX_DIM = 64
DSA_TOPK = 256
MLA_Q_RANK = 256
MLA_KV_RANK = 128
MLA_NOPE = 64
MLA_ROPE = 32
MLA_V = 64
ROPE_THETA = 10000.0
N_BUCKETS = 32
MAX_DISTANCE = 2048
N_EXPERTS = 64
TOP_K = 8
EXPERT_FF = 256
ROUTED_SCALE = 2.5
LN_EPS = 1e-5
RMS_EPS = 1e-6

LANES = 128
NEG = -1e30
INT_MIN = -2147483648
VMEM_LIMIT = 56 * 1024 * 1024
BQ = 256

WA_COLS = 3 * N_HEADS_DIL * HEAD_DIM + 3 * N_HEADS_DSA * HEAD_DIM + IDX_HEADS * IDX_DIM + 2 * IDX_DIM
WB_COLS = MLA_Q_RANK + MLA_KV_RANK + LANES
IQ_BLK = 0
SQ_BLK, SK_BLK, SV_BLK = 4, 6, 8
IK_BLK = 10
DQ_BLK, DK_BLK, DV_BLK = 11, 14, 17


def _cparams(*sem):
    return pltpu.CompilerParams(dimension_semantics=sem, vmem_limit_bytes=VMEM_LIMIT)


def _mm_kernel(x_ref, w_ref, o_ref):
    o_ref[...] = jnp.dot(x_ref[...].astype(BF16), w_ref[...], preferred_element_type=F32).astype(o_ref.dtype)


def _matmul(x, w, out_dtype, tm=512):
    T, K = x.shape
    N = w.shape[1]
    return pl.pallas_call(
        _mm_kernel,
        grid=(T // tm,),
        in_specs=[pl.BlockSpec((tm, K), lambda i: (i, 0)), pl.BlockSpec((K, N), lambda i: (0, 0))],
        out_specs=pl.BlockSpec((tm, N), lambda i: (i, 0)),
        out_shape=jax.ShapeDtypeStruct((T, N), out_dtype),
        compiler_params=_cparams("parallel"),
        name="in_proj",
    )(x, w)


def _head_masks():
    lane = lax.broadcasted_iota(jnp.int32, (1, LANES), 1)
    return lane < HEAD_DIM, lane >= HEAD_DIM


def _select_head(q2, head_lanes):
    return jnp.where(head_lanes, q2 * jnp.asarray(HEAD_DIM ** -0.5, BF16), jnp.zeros((), BF16))


def _softmax_pv(s, v):
    m = jnp.max(s, axis=1, keepdims=True)
    p = jnp.exp(s - m)
    l = jnp.sum(p, axis=1, keepdims=True)
    return jnp.dot(p.astype(BF16), v, preferred_element_type=F32) / l


def _nt_dot(a, b):
    return lax.dot_general(a, b, (((1,), (1,)), ((), ())), preferred_element_type=F32)


def _attn_kernel(q_ref, k_ref, v_ref, w_ref, o_ref, *, S, packed, per_head_w):
    lo, hi = _head_masks()
    for i in range(S // BQ):
        n = (i + 1) * BQ
        rows = slice(i * BQ, (i + 1) * BQ)
        v2 = v_ref[0, :n, :]
        outs = []
        for h in range(2):
            if packed:
                qh = _select_head(q_ref[0, rows, :], lo if h == 0 else hi)
                kh = k_ref[0, :n, :]
            else:
                qh = q_ref[0, rows, h * LANES:(h + 1) * LANES]
                kh = k_ref[0, :n, h * LANES:(h + 1) * LANES]
            s = _nt_dot(qh, kh) + w_ref[h if per_head_w else 0, :, S - n:]
            outs.append(_softmax_pv(s, v2))
        o_ref[0, rows, :] = jnp.where(lo, outs[0], outs[1]).astype(o_ref.dtype)


def _attention(q_arr, k_arr, v_arr, w, *, q_blk, k_blk, v_blk, n_pairs, packed, per_head_w, name):
    B, S, _ = q_arr.shape
    qk_w = LANES if packed else 2 * LANES
    wh = 2 if per_head_w else 1
    return pl.pallas_call(
        functools.partial(_attn_kernel, S=S, packed=packed, per_head_w=per_head_w),
        grid=(n_pairs, B),
        in_specs=[
            pl.BlockSpec((1, S, qk_w), lambda p, b: (b, 0, q_blk + p)),
            pl.BlockSpec((1, S, qk_w), lambda p, b: (b, 0, k_blk + p)),
            pl.BlockSpec((1, S, LANES), lambda p, b: (b, 0, v_blk + p)),
            pl.BlockSpec((wh, BQ, S), (lambda p, b: (p, 0, 0)) if per_head_w else (lambda p, b: (0, 0, 0))),
        ],
        out_specs=pl.BlockSpec((1, S, LANES), lambda p, b: (b, 0, p)),
        out_shape=jax.ShapeDtypeStruct((B, S, n_pairs * LANES), BF16),
        compiler_params=_cparams("parallel", "parallel"),
        name=name,
    )(q_arr, k_arr, v_arr, w)


def _count(mask):
    return jnp.sum(jnp.where(mask, 1.0, 0.0).astype(F32), axis=1, keepdims=True)


def _dsa_kernel(sq_ref, sk_ref, sv_ref, iq_ref, ik_ref, iw_ref, w_ref, o_ref, *key_refs, S, n_sel):
    lo, hi = _head_masks()
    idx_bits = (S - 1).bit_length()
    ksel = float(n_sel)
    nq = S // BQ
    for i in range(nq):
        n = (i + 1) * BQ
        rows = slice(i * BQ, (i + 1) * BQ)
        ikn = ik_ref[0, :n, :]
        iwb = iw_ref[0, rows, :]
        score = jnp.zeros((BQ, n), F32)
        for hh in range(IDX_HEADS):
            c = hh // 2
            iqc = jnp.where(lo if hh % 2 == 0 else hi, iq_ref[0, rows, c * LANES:(c + 1) * LANES],
                            jnp.zeros((), BF16))
            score = score + iwb[:, hh:hh + 1] * jnp.maximum(_nt_dot(iqc, ikn), 0.0)
        col = lax.broadcasted_iota(jnp.int32, (BQ, n), 1)
        row = lax.broadcasted_iota(jnp.int32, (BQ, n), 0)
        bits = lax.bitcast_convert_type(score, jnp.int32)
        key = bits ^ ((bits >> 31) & jnp.int32(0x7FFFFFFF))
        key_refs[i][...] = jnp.where(col <= row + i * BQ, key, jnp.int32(INT_MIN))

    taus0 = tuple(jnp.where(_count(key_refs[i][...] >= 0) >= ksel, jnp.int32(0), jnp.int32(INT_MIN))
                  for i in range(nq))

    def tau_step(it, taus):
        bit = jnp.left_shift(jnp.int32(1), 30 - it)
        out = []
        for i in range(nq):
            cand = taus[i] | bit
            out.append(jnp.where(_count(key_refs[i][...] >= cand) >= ksel, cand, taus[i]))
        return tuple(out)

    taus = lax.fori_loop(0, 31, tau_step, taus0)

    for i in range(nq):
        n = (i + 1) * BQ
        rows = slice(i * BQ, (i + 1) * BQ)
        key = key_refs[i][...]
        tau = taus[i]
        col = lax.broadcasted_iota(jnp.int32, (BQ, n), 1)
        gt = key > tau
        eq = key == tau
        need = ksel - _count(gt)
        excess = (_count(eq) > need) & (tau > jnp.int32(INT_MIN))
        any_excess = jnp.max(jnp.where(excess, 1, 0).astype(jnp.int32)) > 0

        def idx_step(it, j, eq=eq, col=col, need=need):
            cand = j + jnp.left_shift(jnp.int32(1), idx_bits - 1 - it)
            return jnp.where(_count(eq & (col < cand)) < need, cand, j)

        jmax = lax.fori_loop(0, jnp.where(any_excess, idx_bits, 0), idx_step, jnp.zeros((BQ, 1), jnp.int32))
        jmax = jnp.where(excess, jmax, jnp.int32(S))
        madd = jnp.where(gt | (eq & (col <= jmax)), 0.0, NEG).astype(F32)

        for p in range(N_HEADS_DSA // 2):
            q2 = sq_ref[0, rows, p * LANES:(p + 1) * LANES]
            k2 = sk_ref[0, :n, p * LANES:(p + 1) * LANES]
            v2 = sv_ref[0, :n, p * LANES:(p + 1) * LANES]
            outs = []
            for h in range(2):
                qh = _select_head(q2, lo if h == 0 else hi)
                s = _nt_dot(qh, k2) + w_ref[2 * p + h, :, S - n:] + madd
                outs.append(_softmax_pv(s, v2))
            o_ref[0, rows, p * LANES:(p + 1) * LANES] = jnp.where(lo, outs[0], outs[1]).astype(o_ref.dtype)


def _dsa(ua3, ub3, w):
    B, S, _ = ua3.shape
    n_sel = min(DSA_TOPK, S // 4)
    assert n_sel <= BQ
    hw = N_HEADS_DSA * HEAD_DIM
    return pl.pallas_call(
        functools.partial(_dsa_kernel, S=S, n_sel=n_sel),
        grid=(B,),
        in_specs=[
            pl.BlockSpec((1, S, hw), lambda b: (b, 0, SQ_BLK * LANES // hw)),
            pl.BlockSpec((1, S, hw), lambda b: (b, 0, SK_BLK * LANES // hw)),
            pl.BlockSpec((1, S, hw), lambda b: (b, 0, SV_BLK * LANES // hw)),
            pl.BlockSpec((1, S, IDX_HEADS * IDX_DIM), lambda b: (b, 0, IQ_BLK * LANES // (IDX_HEADS * IDX_DIM))),
            pl.BlockSpec((1, S, LANES), lambda b: (b, 0, IK_BLK)),
            pl.BlockSpec((1, S, LANES), lambda b: (b, 0, (MLA_Q_RANK + MLA_KV_RANK) // LANES)),
            pl.BlockSpec((N_HEADS_DSA, BQ, S), lambda b: (0, 0, 0)),
        ],
        out_specs=pl.BlockSpec((1, S, hw), lambda b: (b, 0, 0)),
        out_shape=jax.ShapeDtypeStruct((B, S, hw), BF16),
        scratch_shapes=[pltpu.VMEM((BQ, (i + 1) * BQ), jnp.int32) for i in range(S // BQ)],
        compiler_params=_cparams("parallel"),
        name="dsa_attn",
    )(ua3, ua3, ua3, ua3, ua3, ub3, w)


def _mla_prep_kernel(ub_ref, qn_ref, kvn_ref, wq_ref, wk_ref, wv_ref, c_ref, s1_ref, s2_ref,
                     q_ref, k_ref, v_ref, *, scale):
    ub = ub_ref[...]
    cq = ub[:, :MLA_Q_RANK]
    ckv = ub[:, MLA_Q_RANK:MLA_Q_RANK + MLA_KV_RANK]
    kr_chunk = ub[:, MLA_Q_RANK + MLA_KV_RANK:]

    def rms(t, g):
        return t * lax.rsqrt(jnp.mean(t * t, axis=-1, keepdims=True) + RMS_EPS) * g

    qn = rms(cq, qn_ref[...]).astype(BF16)
    kvn = rms(ckv, kvn_ref[...]).astype(BF16)
    q = jnp.dot(qn, wq_ref[...], preferred_element_type=F32)
    kn = jnp.dot(kvn, wk_ref[...], preferred_element_type=F32)
    v = jnp.dot(kvn, wv_ref[...], preferred_element_type=F32)
    c, s1, s2 = c_ref[...], s1_ref[...], s2_ref[...]

    def rope(t):
        return t * c + pltpu.roll(t, 16, 1) * s1 + pltpu.roll(t, LANES - 16, 1) * s2

    lane = lax.broadcasted_iota(jnp.int32, (1, LANES), 1)
    kr = jnp.where((lane >= MLA_NOPE) & (lane < MLA_NOPE + MLA_ROPE), rope(kr_chunk), 0.0)
    for h in range(N_HEADS_MLA):
        cols = slice(h * LANES, (h + 1) * LANES)
        q_ref[:, cols] = (rope(q[:, cols]) * scale).astype(BF16)
        k_ref[:, cols] = (kn[:, cols] + kr).astype(BF16)
    v_ref[...] = v.astype(BF16)


def _mla_prep(ub, qn, kvn, wq, wk, wv, c, s1, s2, tm=512):
    T = ub.shape[0]
    scale = (MLA_NOPE + MLA_ROPE) ** -0.5
    row = lambda i: (i, 0)
    fix = lambda i: (0, 0)
    hq = N_HEADS_MLA * LANES
    hv = N_HEADS_MLA * MLA_V
    return pl.pallas_call(
        functools.partial(_mla_prep_kernel, scale=scale),
        grid=(T // tm,),
        in_specs=[
            pl.BlockSpec((tm, WB_COLS), row),
            pl.BlockSpec((1, MLA_Q_RANK), fix), pl.BlockSpec((1, MLA_KV_RANK), fix),
            pl.BlockSpec((MLA_Q_RANK, hq), fix), pl.BlockSpec((MLA_KV_RANK, hq), fix),
            pl.BlockSpec((MLA_KV_RANK, hv), fix),
            pl.BlockSpec((tm, LANES), row), pl.BlockSpec((tm, LANES), row), pl.BlockSpec((tm, LANES), row),
        ],
        out_specs=[pl.BlockSpec((tm, hq), row), pl.BlockSpec((tm, hq), row), pl.BlockSpec((tm, hv), row)],
        out_shape=[jax.ShapeDtypeStruct((T, hq), BF16), jax.ShapeDtypeStruct((T, hq), BF16),
                   jax.ShapeDtypeStruct((T, hv), BF16)],
        compiler_params=_cparams("parallel"),
        name="mla_prep",
    )(ub, qn, kvn, wq, wk, wv, c, s1, s2)


def _layer_norm(y, g, b):
    mu = jnp.mean(y, axis=-1, keepdims=True)
    yc = y - mu
    var = jnp.mean(yc * yc, axis=-1, keepdims=True)
    return yc * lax.rsqrt(var + LN_EPS) * g + b


def _outproj_kernel(oa_ref, ob_ref, oc_ref, wa_ref, wb_ref, wc_ref, x_ref, g_ref, b_ref, o_ref, *, alpha):
    acc = jnp.dot(oa_ref[...], wa_ref[...], preferred_element_type=F32)
    acc += jnp.dot(ob_ref[...], wb_ref[...], preferred_element_type=F32)
    acc += jnp.dot(oc_ref[...], wc_ref[...], preferred_element_type=F32)
    o_ref[...] = _layer_norm(alpha * x_ref[...] + acc, g_ref[...], b_ref[...])


def _outproj_ln(oa, ob, oc, wa, wb, wc, x, g, b, alpha, tm=512):
    T, D = x.shape
    row = lambda i: (i, 0)
    fix = lambda i: (0, 0)
    return pl.pallas_call(
        functools.partial(_outproj_kernel, alpha=alpha),
        grid=(T // tm,),
        in_specs=[pl.BlockSpec((tm, oa.shape[1]), row), pl.BlockSpec((tm, ob.shape[1]), row),
                  pl.BlockSpec((tm, oc.shape[1]), row),
                  pl.BlockSpec(wa.shape, fix), pl.BlockSpec(wb.shape, fix), pl.BlockSpec(wc.shape, fix),
                  pl.BlockSpec((tm, D), row), pl.BlockSpec((1, D), fix), pl.BlockSpec((1, D), fix)],
        out_specs=pl.BlockSpec((tm, D), row),
        out_shape=jax.ShapeDtypeStruct((T, D), F32),
        compiler_params=_cparams("parallel"),
        name="out_proj_ln",
    )(oa, ob, oc, wa, wb, wc, x, g, b)


MOE_TT = 256
MOE_CH = 8
MOE_BM = 1024
MOE_SLOTS = -(-(TOP_K * MOE_TT + N_EXPERTS * (MOE_CH - 1)) // MOE_TT) * MOE_TT
MOE_MAXCH = MOE_SLOTS // MOE_CH
MOE_ZROWS = 256
HALF = D_MODEL // 2
HI_MASK = -65536


def _pack_bf16_pairs(t):
    lo = lax.bitcast_convert_type(t[:, :HALF], jnp.int32)
    hi = lax.bitcast_convert_type(t[:, HALF:], jnp.int32)
    return (hi & jnp.int32(HI_MASK)) | lax.shift_right_logical(lo, jnp.int32(16))


def _unpack_bf16_pairs(w):
    lo = lax.bitcast_convert_type(lax.shift_left(w, jnp.int32(16)), F32)
    hi = lax.bitcast_convert_type(w & jnp.int32(HI_MASK), F32)
    return jnp.concatenate([lo, hi], axis=1).astype(BF16)


def _route_kernel(x_ref, rw_ref, rb_ref, slot_ref, gate_ref, slott_ref, meta_ref):
    tt = x_ref.shape[0]
    logits = jnp.dot(x_ref[...].astype(BF16), rw_ref[...], preferred_element_type=F32)
    scores = 1.0 / (1.0 + jnp.exp(-logits))
    lane = lax.broadcasted_iota(jnp.int32, logits.shape, 1)
    lane_f = lane.astype(F32)
    work = jnp.where(lane < N_EXPERTS, scores + rb_ref[...], NEG)
    sel = lane < 0
    firsts = []
    for _ in range(TOP_K):
        m = jnp.max(work, axis=1, keepdims=True)
        first = jnp.min(jnp.where(work == m, lane_f, float(LANES)), axis=1, keepdims=True)
        pick = lane_f == first
        sel = sel | pick
        work = jnp.where(pick, NEG, work)
        firsts.append(first)
    gs = jnp.where(sel, scores, 0.0)
    gates = gs / jnp.sum(gs, axis=1, keepdims=True) * ROUTED_SCALE

    sel_f = jnp.where(sel, 1.0, 0.0).astype(F32)
    r_i = lax.broadcasted_iota(jnp.int32, (tt, tt), 0)
    c_i = lax.broadcasted_iota(jnp.int32, (tt, tt), 1)
    below = jnp.where(r_i > c_i, 1.0, 0.0).astype(BF16)
    rank = jnp.dot(below, sel_f.astype(BF16), preferred_element_type=F32)
    cnt = jnp.sum(sel_f, axis=0, keepdims=True)
    cnt_pad = jnp.floor((cnt + (MOE_CH - 1)) * (1.0 / MOE_CH)) * MOE_CH
    e_r = lax.broadcasted_iota(jnp.int32, (LANES, LANES), 0)
    e_c = lax.broadcasted_iota(jnp.int32, (LANES, LANES), 1)
    before = jnp.where(e_r < e_c, 1.0, 0.0).astype(BF16)
    base = jnp.dot(jnp.broadcast_to(cnt_pad, (8, LANES)).astype(BF16), before,
                   preferred_element_type=F32)[0:1]
    slot = base + rank

    slot_tk = jnp.full(logits.shape, -1.0, F32)
    gate_tk = jnp.zeros(logits.shape, F32)
    for k, first in enumerate(firsts):
        pick = lane_f == first
        slot_k = jnp.sum(jnp.where(pick, slot, 0.0), axis=1, keepdims=True)
        gate_k = jnp.sum(jnp.where(pick, gates, 0.0), axis=1, keepdims=True)
        slot_tk = jnp.where(lane == k, slot_k, slot_tk)
        gate_tk = jnp.where(lane == k, gate_k, gate_tk)
    slot_ref[...] = slot_tk.astype(jnp.int32)
    gate_ref[...] = gate_tk
    slott_ref[0] = slot_tk.T[:TOP_K].astype(jnp.int32)
    row = lax.broadcasted_iota(jnp.int32, (8, LANES), 0)
    meta = jnp.where(row == 0, jnp.broadcast_to(cnt_pad, (8, LANES)), jnp.broadcast_to(base, (8, LANES)))
    meta_ref[0] = meta.astype(jnp.int32)


def _route(x, rw, rb):
    T, D = x.shape
    nt = T // MOE_TT
    return pl.pallas_call(
        _route_kernel,
        grid=(nt,),
        in_specs=[pl.BlockSpec((MOE_TT, D), lambda i: (i, 0)), pl.BlockSpec((D, LANES), lambda i: (0, 0)),
                  pl.BlockSpec((1, LANES), lambda i: (0, 0))],
        out_specs=[pl.BlockSpec((MOE_TT, LANES), lambda i: (i, 0)), pl.BlockSpec((MOE_TT, LANES), lambda i: (i, 0)),
                   pl.BlockSpec((1, TOP_K, MOE_TT), lambda i: (i, 0, 0)), pl.BlockSpec((1, 8, LANES), lambda i: (i, 0, 0))],
        out_shape=[jax.ShapeDtypeStruct((T, LANES), jnp.int32), jax.ShapeDtypeStruct((T, LANES), F32),
                   jax.ShapeDtypeStruct((nt, TOP_K, MOE_TT), jnp.int32), jax.ShapeDtypeStruct((nt, 8, LANES), jnp.int32)],
        compiler_params=_cparams("parallel"),
        name="moe_route",
    )(x, rw, rb)


def _chunk_copy(src_ref, dst_ref, sem):
    return pltpu.make_async_copy(src_ref, dst_ref, sem)


def _wait_n(n, wait_one):
    def eight(_, c):
        for _ in range(8):
            wait_one()
        return c

    def one(_, c):
        wait_one()
        return c
    lax.fori_loop(0, lax.shift_right_logical(n, 3), eight, 0)
    lax.fori_loop(0, n & 7, one, 0)


def _dispatch_kernel(dst_ref, tchunks_ref, tail_start_ref, tail_n_ref,
                     x_ref, slott_ref, xs_ref, loc_ref, zero_ref, sem, tail_sem):
    i = pl.program_id(0)
    nt = pl.num_programs(0)
    cur = i % 2
    s_iota = lax.broadcasted_iota(jnp.int32, (MOE_SLOTS, MOE_TT), 0)
    hit = s_iota == slott_ref[0, 0:1, :]
    for k in range(1, TOP_K):
        hit = hit | (s_iota == slott_ref[0, k:k + 1, :])
    onehot = jnp.where(hit, 1.0, 0.0).astype(BF16)
    rows = jnp.dot(onehot, x_ref[...].astype(BF16), preferred_element_type=F32)
    loc_ref[cur] = _pack_bf16_pairs(rows)

    def wait_chunks(n):
        _wait_n(n, lambda: _chunk_copy(loc_ref.at[0, pl.ds(0, MOE_CH), :], xs_ref.at[pl.ds(0, MOE_CH), :],
                                       sem).wait())

    @pl.when(i > 0)
    def _():
        wait_chunks(tchunks_ref[i - 1])

    def per_chunk(q, c):
        src = loc_ref.at[cur, pl.ds(pl.multiple_of(q * MOE_CH, MOE_CH), MOE_CH), :]
        dst = xs_ref.at[pl.ds(pl.multiple_of(dst_ref[i * MOE_MAXCH + q], MOE_CH), MOE_CH), :]
        _chunk_copy(src, dst, sem).start()
        return c

    lax.fori_loop(0, tchunks_ref[i], per_chunk, 0)

    @pl.when(i == nt - 1)
    def _():
        zero_ref[...] = jnp.zeros_like(zero_ref)

        def fill(rows_per_copy, first, last):
            z = zero_ref.at[pl.ds(0, rows_per_copy), :]

            def region(e, c):
                g = tail_start_ref[e]

                def start(j, c2):
                    row = pl.multiple_of(g + j * rows_per_copy, MOE_CH)
                    _chunk_copy(z, xs_ref.at[pl.ds(row, rows_per_copy), :], tail_sem).start()
                    return c2
                return lax.fori_loop(0, tail_n_ref[e], start, c)
            lax.fori_loop(first, last, region, 0)

            def region_wait(e, c):
                _wait_n(tail_n_ref[e], lambda: _chunk_copy(z, xs_ref.at[pl.ds(0, rows_per_copy), :],
                                                           tail_sem).wait())
                return c
            lax.fori_loop(first, last, region_wait, 0)

        fill(MOE_CH, 0, N_EXPERTS)
        fill(MOE_ZROWS, N_EXPERTS, N_EXPERTS + 1)
        wait_chunks(tchunks_ref[i])


def _dispatch(x, slot_kt, tables, n_rows):
    T, D = x.shape
    nt = T // MOE_TT
    dst, tchunks, tail_start, tail_n = tables
    return pl.pallas_call(
        _dispatch_kernel,
        grid_spec=pltpu.PrefetchScalarGridSpec(
            num_scalar_prefetch=4,
            grid=(nt,),
            in_specs=[pl.BlockSpec((MOE_TT, D), lambda i, *_: (i, 0)),
                      pl.BlockSpec((1, TOP_K, MOE_TT), lambda i, *_: (i, 0, 0))],
            out_specs=pl.BlockSpec(memory_space=pl.ANY),
            scratch_shapes=[pltpu.VMEM((2, MOE_SLOTS, HALF), jnp.int32), pltpu.VMEM((MOE_ZROWS, HALF), jnp.int32),
                            pltpu.SemaphoreType.DMA(()), pltpu.SemaphoreType.DMA(())],
        ),
        out_shape=jax.ShapeDtypeStruct((n_rows, HALF), jnp.int32),
        compiler_params=_cparams("arbitrary"),
        name="moe_dispatch",
    )(dst, tchunks, tail_start, tail_n, x, slot_kt)


def _expert_kernel(be_ref, nu_ref, xs_ref, w1_ref, w3_ref, w2_ref, ys_ref, w1b_ref, w3b_ref, w2b_ref):
    i = pl.program_id(0)

    @pl.when((i == 0) | (be_ref[i] != be_ref[jnp.maximum(i - 1, 0)]))
    def _():
        w1b_ref[...] = w1_ref[0].astype(BF16)
        w3b_ref[...] = w3_ref[0].astype(BF16)
        w2b_ref[...] = w2_ref[0].astype(BF16)

    @pl.when(i < nu_ref[0])
    def _():
        xb = _unpack_bf16_pairs(xs_ref[...])
        h1 = jnp.dot(xb, w1b_ref[...], preferred_element_type=F32)
        h3 = jnp.dot(xb, w3b_ref[...], preferred_element_type=F32)
        h = h1 / (1.0 + jnp.exp(-h1)) * h3
        y = jnp.dot(h.astype(BF16), w2b_ref[...], preferred_element_type=F32)
        ys_ref[...] = _pack_bf16_pairs(y.astype(BF16).astype(F32))

    @pl.when(i >= nu_ref[0])
    def _():
        ys_ref[...] = jnp.zeros_like(ys_ref)


def _experts(xs, block_expert, n_used, w1, w3, w2):
    n_rows = xs.shape[0]
    nb = n_rows // MOE_BM
    _, D, ff = w1.shape
    blk = lambda i, be, nu: (jnp.minimum(i, nu[0] - 1), 0)
    return pl.pallas_call(
        _expert_kernel,
        grid_spec=pltpu.PrefetchScalarGridSpec(
            num_scalar_prefetch=2,
            grid=(nb,),
            in_specs=[pl.BlockSpec((MOE_BM, HALF), blk),
                      pl.BlockSpec((1, D, ff), lambda i, be, nu: (be[i], 0, 0)),
                      pl.BlockSpec((1, D, ff), lambda i, be, nu: (be[i], 0, 0)),
                      pl.BlockSpec((1, ff, D), lambda i, be, nu: (be[i], 0, 0))],
            out_specs=pl.BlockSpec((MOE_BM, HALF), lambda i, be, nu: (i, 0)),
            scratch_shapes=[pltpu.VMEM((D, ff), BF16), pltpu.VMEM((D, ff), BF16), pltpu.VMEM((ff, D), BF16)],
        ),
        out_shape=jax.ShapeDtypeStruct((n_rows, HALF), jnp.int32),
        compiler_params=_cparams("arbitrary"),
        name="moe_experts",
    )(block_expert, n_used, xs, w1, w3, w2)


def _combine_kernel(src_ref, tchunks_ref, slot_ref, gate_ref, ys_ref, o_ref, loc_ref, sem):
    i = pl.program_id(0)
    nt = pl.num_programs(0)
    cur = i % 2

    def fetch(t, buf):
        def per_chunk(q, c):
            src = ys_ref.at[pl.ds(pl.multiple_of(src_ref[t * MOE_MAXCH + q], MOE_CH), MOE_CH), :]
            dst = loc_ref.at[buf, pl.ds(pl.multiple_of(q * MOE_CH, MOE_CH), MOE_CH), :]
            _chunk_copy(src, dst, sem.at[buf]).start()
            return c
        lax.fori_loop(0, tchunks_ref[t], per_chunk, 0)

    @pl.when(i == 0)
    def _():
        loc_ref[...] = jnp.zeros_like(loc_ref)
        fetch(0, 0)

    @pl.when(i + 1 < nt)
    def _():
        fetch(i + 1, 1 - cur)

    _wait_n(tchunks_ref[i], lambda: _chunk_copy(ys_ref.at[pl.ds(0, MOE_CH), :],
                                                loc_ref.at[cur, pl.ds(0, MOE_CH), :], sem.at[cur]).wait())

    s_iota = lax.broadcasted_iota(jnp.int32, (MOE_TT, MOE_SLOTS), 1)
    slots = slot_ref[...]
    gates = gate_ref[...]
    wgt = jnp.zeros((MOE_TT, MOE_SLOTS), F32)
    for k in range(TOP_K):
        wgt = jnp.where(s_iota == slots[:, k:k + 1], gates[:, k:k + 1], wgt)
    w_hi = wgt.astype(BF16)
    w_lo = (wgt - w_hi.astype(F32)).astype(BF16)
    ys = _unpack_bf16_pairs(loc_ref[cur])
    o_ref[...] = (jnp.dot(w_hi, ys, preferred_element_type=F32) + jnp.dot(w_lo, ys, preferred_element_type=F32))


def _combine(ys, slot_tk, gate_tk, tables):
    T = slot_tk.shape[0]
    nt = T // MOE_TT
    src, tchunks = tables
    return pl.pallas_call(
        _combine_kernel,
        grid_spec=pltpu.PrefetchScalarGridSpec(
            num_scalar_prefetch=2,
            grid=(nt,),
            in_specs=[pl.BlockSpec((MOE_TT, LANES), lambda i, *_: (i, 0)),
                      pl.BlockSpec((MOE_TT, LANES), lambda i, *_: (i, 0)),
                      pl.BlockSpec(memory_space=pl.ANY)],
            out_specs=pl.BlockSpec((MOE_TT, D_MODEL), lambda i, *_: (i, 0)),
            scratch_shapes=[pltpu.VMEM((2, MOE_SLOTS, HALF), jnp.int32), pltpu.SemaphoreType.DMA((2,))],
        ),
        out_shape=jax.ShapeDtypeStruct((T, D_MODEL), F32),
        compiler_params=_cparams("arbitrary"),
        name="moe_combine",
    )(src, tchunks, slot_tk, gate_tk, ys)


def _shared_ln_kernel(x_ref, r_ref, w1_ref, w3_ref, w2_ref, g_ref, b_ref, o_ref, *, alpha):
    x = x_ref[...]
    xb = x.astype(BF16)
    h1 = jnp.dot(xb, w1_ref[...], preferred_element_type=F32)
    h3 = jnp.dot(xb, w3_ref[...], preferred_element_type=F32)
    h = h1 / (1.0 + jnp.exp(-h1)) * h3
    y = jnp.dot(h.astype(BF16), w2_ref[...], preferred_element_type=F32)
    o_ref[...] = _layer_norm(alpha * x + (r_ref[...] + y), g_ref[...], b_ref[...])


def _shared_ln(x, routed, w1, w3, w2, g, b, alpha, tm=512):
    T, D = x.shape
    row = lambda i: (i, 0)
    fix = lambda i: (0, 0)
    return pl.pallas_call(
        functools.partial(_shared_ln_kernel, alpha=alpha),
        grid=(T // tm,),
        in_specs=[pl.BlockSpec((tm, D), row), pl.BlockSpec((tm, D), row), pl.BlockSpec(w1.shape, fix),
                  pl.BlockSpec(w3.shape, fix), pl.BlockSpec(w2.shape, fix), pl.BlockSpec((1, D), fix),
                  pl.BlockSpec((1, D), fix)],
        out_specs=pl.BlockSpec((tm, D), row),
        out_shape=jax.ShapeDtypeStruct((T, D), F32),
        compiler_params=_cparams("parallel"),
        name="shared_ffn_ln",
    )(x, routed, w1, w3, w2, g, b)


def _dispatch_tables(meta, n_blocks):
    cnt = meta[:, 0, :N_EXPERTS]
    base = meta[:, 1, :N_EXPERTS]
    tot = jnp.sum(cnt, axis=0)
    nblk = (tot + MOE_BM - 1) // MOE_BM
    bend = jnp.cumsum(nblk)
    pstart = (bend - nblk) * MOE_BM
    gstart = pstart[None, :] + jnp.cumsum(cnt, axis=0) - cnt
    n_used = bend[-1:]
    blocks = jnp.arange(n_blocks, dtype=jnp.int32)
    be = jnp.sum((jnp.minimum(blocks, n_used - 1)[:, None] >= bend[None, :]).astype(jnp.int32), axis=1)
    q_row = jnp.arange(MOE_MAXCH, dtype=jnp.int32) * MOE_CH
    owner = jnp.sum((base[:, None, :] <= q_row[None, :, None]).astype(jnp.int32), axis=-1) - 1
    dst = jnp.take_along_axis(gstart - base, owner, axis=1) + q_row[None, :]
    i32 = lambda a: a.astype(jnp.int32)
    common = (i32(dst.reshape(-1)), i32(jnp.sum(cnt, axis=1) // MOE_CH))
    tails = (i32(jnp.concatenate([pstart + tot, n_used * MOE_BM])),
             i32(jnp.concatenate([(nblk * MOE_BM - tot) // MOE_CH, (n_blocks - n_used) * (MOE_BM // MOE_ZROWS)])))
    return common, tails, i32(jnp.minimum(be, N_EXPERTS - 1)), i32(n_used)


def _moe_block(x1, rw, rb, w1, w3, w2, sw1, sw3, sw2, lg, lb, alpha):
    T = x1.shape[0]
    nt = T // MOE_TT
    n_blocks = -(-(TOP_K * T + nt * N_EXPERTS * (MOE_CH - 1)) // MOE_BM) + N_EXPERTS
    slot_tk, gate_tk, slot_kt, meta = _route(x1, rw, rb)
    common, tails, be, n_used = _dispatch_tables(meta, n_blocks)
    xs = _dispatch(x1, slot_kt, common + tails, n_blocks * MOE_BM)
    ys = _experts(xs, be, n_used, w1, w3, w2)
    routed = _combine(ys, slot_tk, gate_tk, common)
    return _shared_ln(x1, routed, sw1, sw3, sw2, lg, lb, alpha)


def _t5_bucket(dist):
    max_exact = N_BUCKETS // 2
    d_f = jnp.maximum(dist, 1).astype(F32)
    large = max_exact + (jnp.log(d_f / max_exact) / math.log(MAX_DISTANCE / max_exact)
                         * (N_BUCKETS - max_exact)).astype(jnp.int32)
    large = jnp.minimum(large, N_BUCKETS - 1)
    return jnp.where(dist < max_exact, dist, large)


def _bias_tables(rel_bias, S):
    dist = np.arange(S)
    e = rel_bias[_t5_bucket(jnp.asarray(dist, jnp.int32))].T
    mult = sum(((dist % d == 0) & (dist <= w)).astype(np.float32) for w, d in DILATIONS)
    e_dil = jnp.where(mult > 0, e[:N_HEADS_DIL] + jnp.log(jnp.maximum(mult, 1.0)), NEG)
    e_all = jnp.concatenate([e_dil, e[N_HEADS_DIL:], jnp.zeros((1, S), F32)], 0)
    period = S + BQ
    n_h = e_all.shape[0]
    a = jnp.concatenate([e_all[:, :S - BQ + 1][:, ::-1], jnp.full((n_h, BQ), NEG, F32),
                         e_all[:, S - BQ + 1:][:, ::-1]], 1)
    w = jnp.tile(a, (1, BQ))[:, :BQ * (period - 1)].reshape(n_h, BQ, period - 1)[:, :, :S].astype(F32)
    return w[:N_HEADS_DIL], w[N_HEADS_DIL:N_HEADS_DIL + N_HEADS_DSA], w[-1:]


def _rope_tables(positions):
    inv = ROPE_THETA ** (-jnp.arange(0, MLA_ROPE, 2, dtype=F32) / MLA_ROPE)
    ang = positions.astype(F32).reshape(-1, 1) * inv
    cos, sin = jnp.cos(ang), jnp.sin(ang)
    T, half = cos.shape
    one = jnp.ones((T, MLA_NOPE), F32)
    z = lambda n: jnp.zeros((T, n), F32)
    tail = LANES - MLA_NOPE - MLA_ROPE
    c = jnp.concatenate([one, cos, cos, jnp.ones((T, tail), F32)], 1)
    s1 = jnp.concatenate([z(MLA_NOPE + half), sin, z(tail)], 1)
    s2 = jnp.concatenate([z(MLA_NOPE), -sin, z(half + tail)], 1)
    return c, s1, s2


def _split_w_in(w):
    o = np.cumsum([0, 384, 384, 384, 256, 256, 256, 512, 64, 8, 256, 128, 32])
    ik = w[:, o[7]:o[8]]
    wa = jnp.concatenate([w[:, o[6]:o[7]], w[:, o[3]:o[6]], ik, ik, w[:, :o[3]]], 1)
    z = lambda n: jnp.zeros((w.shape[0], n), w.dtype)
    chunk = jnp.concatenate([w[:, o[8]:o[9]], z(MLA_NOPE - IDX_HEADS), w[:, o[11]:o[12]],
                             z(LANES - MLA_NOPE - MLA_ROPE)], 1)
    wb = jnp.concatenate([w[:, o[9]:o[11]], chunk], 1)
    return wa.astype(BF16), wb.astype(BF16)


def _split_mla_weights(w_uq, w_ukv):
    r = w_uq.shape[0]
    wq = w_uq.reshape(r, N_HEADS_MLA, MLA_NOPE + MLA_ROPE)
    wq = jnp.pad(wq, ((0, 0), (0, 0), (0, LANES - MLA_NOPE - MLA_ROPE))).reshape(r, N_HEADS_MLA * LANES)
    r = w_ukv.shape[0]
    wkv = w_ukv.reshape(r, N_HEADS_MLA, MLA_NOPE + MLA_V)
    wk = jnp.pad(wkv[:, :, :MLA_NOPE], ((0, 0), (0, 0), (0, LANES - MLA_NOPE))).reshape(r, N_HEADS_MLA * LANES)
    wv = wkv[:, :, MLA_NOPE:].reshape(r, N_HEADS_MLA * MLA_V)
    return wq.astype(BF16), wk.astype(BF16), wv.astype(BF16)


def kernel(x, positions, w_in, mla_q_norm, mla_w_uq, mla_kv_norm, mla_w_ukv, w_out, ln1_g, ln1_b, router_w,
           router_bias, exp_w1, exp_w3, exp_w2, sh_w1, sh_w3, sh_w2, ln2_g, ln2_b, rel_bias):
    B, S, D = x.shape
    T = B * S
    depth = w_in.shape[0]
    alpha = (2 * depth) ** 0.25
    w_dil, w_dsa, w_causal = _bias_tables(rel_bias, S)
    rope_c, rope_s1, rope_s2 = _rope_tables(positions)
    na, nb = N_HEADS_DIL * HEAD_DIM, (N_HEADS_DIL + N_HEADS_DSA) * HEAD_DIM
    xf = x.reshape(T, D)
    for l in range(depth):
        wa, wb = _split_w_in(w_in[l])
        ua = _matmul(xf, wa, BF16)
        ub = _matmul(xf, wb, F32)
        ua3 = ua.reshape(B, S, WA_COLS)
        o_a = _attention(ua3, ua3, ua3, w_dil, q_blk=DQ_BLK, k_blk=DK_BLK, v_blk=DV_BLK,
                         n_pairs=N_HEADS_DIL // 2, packed=True, per_head_w=True, name="dilated_attn")
        o_b = _dsa(ua3, ub.reshape(B, S, WB_COLS), w_dsa)
        wq, wk, wv = _split_mla_weights(mla_w_uq[l], mla_w_ukv[l])
        q, k, v = _mla_prep(ub, mla_q_norm[l][None], mla_kv_norm[l][None], wq, wk, wv, rope_c, rope_s1, rope_s2)
        o_c = _attention(q.reshape(B, S, -1), k.reshape(B, S, -1), v.reshape(B, S, -1), w_causal,
                         q_blk=0, k_blk=0, v_blk=0, n_pairs=N_HEADS_MLA // 2, packed=False, per_head_w=False,
                         name="mla_attn")
        wo = w_out[l].astype(BF16)
        x1 = _outproj_ln(o_a.reshape(T, -1), o_b.reshape(T, -1), o_c.reshape(T, -1),
                         wo[:na], wo[na:nb], wo[nb:], xf, ln1_g[l][None], ln1_b[l][None], alpha)
        rw = jnp.pad(router_w[l], ((0, 0), (0, LANES - N_EXPERTS))).astype(BF16)
        rb = jnp.pad(router_bias[l].astype(F32), (0, LANES - N_EXPERTS))[None]
        xf = _moe_block(x1, rw, rb, exp_w1[l], exp_w3[l], exp_w2[l], sh_w1[l].astype(BF16), sh_w3[l].astype(BF16),
                        sh_w2[l].astype(BF16), ln2_g[l][None], ln2_b[l][None], alpha)
    return xf.reshape(B, S, D)
```

```python
import functools
import math

import jax
import jax.numpy as jnp
import numpy as np
from jax import lax
from jax.experimental import pallas as pl
from jax.experimental.pallas import tpu as pltpu

F32 = jnp.float32
BF16 = jnp.bfloat16

D_MODEL = 1024
HEAD_DIM = 64
N_HEADS_DIL = 6
N_HEADS_DSA = 4
N_HEADS_MLA = 6
DILATIONS = ((128, 1), (512, 4), (2048, 16))
IDX_HEADS = 8
IDX_DIM = 64
DSA_TOPK = 256
MLA_Q_RANK = 256
MLA_KV_RANK = 128
MLA_NOPE = 64
MLA_ROPE = 32
MLA_V = 64
ROPE_THETA = 10000.0
N_BUCKETS = 32
MAX_DISTANCE = 2048
N_EXPERTS = 64
TOP_K = 8
EXPERT_FF = 256
ROUTED_SCALE = 2.5
LN_EPS = 1e-5
RMS_EPS = 1e-6

LANES = 128
NEG = -1e30
INT_MIN = -2147483648
VMEM_LIMIT = 56 * 1024 * 1024
BQ = 256

WA_COLS = 3 * N_HEADS_DIL * HEAD_DIM + 3 * N_HEADS_DSA * HEAD_DIM + IDX_HEADS * IDX_DIM + 2 * IDX_DIM
WB_COLS = MLA_Q_RANK + MLA_KV_RANK + LANES
IQ_BLK = 0
SQ_BLK, SK_BLK, SV_BLK = 4, 6, 8
IK_BLK = 10
DQ_BLK, DK_BLK, DV_BLK = 11, 14, 17


def _cparams(*sem):
    return pltpu.CompilerParams(dimension_semantics=sem, vmem_limit_bytes=VMEM_LIMIT)


def _mm_kernel(x_ref, w_ref, o_ref):
    o_ref[...] = jnp.dot(x_ref[...].astype(BF16), w_ref[...], preferred_element_type=F32).astype(o_ref.dtype)


def _matmul(x, w, out_dtype, tm=512):
    T, K = x.shape
    N = w.shape[1]
    return pl.pallas_call(
        _mm_kernel,
        grid=(T // tm,),
        in_specs=[pl.BlockSpec((tm, K), lambda i: (i, 0)), pl.BlockSpec((K, N), lambda i: (0, 0))],
        out_specs=pl.BlockSpec((tm, N), lambda i: (i, 0)),
        out_shape=jax.ShapeDtypeStruct((T, N), out_dtype),
        compiler_params=_cparams("parallel"),
        name="in_proj",
    )(x, w)


def _head_masks():
    lane = lax.broadcasted_iota(jnp.int32, (1, LANES), 1)
    return lane < HEAD_DIM, lane >= HEAD_DIM


def _select_head(q2, head_lanes):
    return jnp.where(head_lanes, q2 * jnp.asarray(HEAD_DIM ** -0.5, BF16), jnp.zeros((), BF16))


def _softmax_pv(s, v):
    m = jnp.max(s, axis=1, keepdims=True)
    p = jnp.exp(s - m)
    l = jnp.sum(p, axis=1, keepdims=True)
    return jnp.dot(p.astype(BF16), v, preferred_element_type=F32) / l


def _nt_dot(a, b):
    return lax.dot_general(a, b, (((1,), (1,)), ((), ())), preferred_element_type=F32)


def _attn_kernel(q_ref, k_ref, v_ref, w_ref, o_ref, *, S, packed, per_head_w):
    lo, hi = _head_masks()
    for i in range(S // BQ):
        n = (i + 1) * BQ
        rows = slice(i * BQ, (i + 1) * BQ)
        v2 = v_ref[0, :n, :]
        outs = []
        for h in range(2):
            if packed:
                qh = _select_head(q_ref[0, rows, :], lo if h == 0 else hi)
                kh = k_ref[0, :n, :]
            else:
                qh = q_ref[0, rows, h * LANES:(h + 1) * LANES]
                kh = k_ref[0, :n, h * LANES:(h + 1) * LANES]
            s = _nt_dot(qh, kh) + w_ref[h if per_head_w else 0, :, S - n:]
            outs.append(_softmax_pv(s, v2))
        o_ref[0, rows, :] = jnp.where(lo, outs[0], outs[1]).astype(o_ref.dtype)


def _attention(q_arr, k_arr, v_arr, w, *, q_blk, k_blk, v_blk, n_pairs, packed, per_head_w, name):
    B, S, _ = q_arr.shape
    qk_w = LANES if packed else 2 * LANES
    wh = 2 if per_head_w else 1
    return pl.pallas_call(
        functools.partial(_attn_kernel, S=S, packed=packed, per_head_w=per_head_w),
        grid=(n_pairs, B),
        in_specs=[
            pl.BlockSpec((1, S, qk_w), lambda p, b: (b, 0, q_blk + p)),
            pl.BlockSpec((1, S, qk_w), lambda p, b: (b, 0, k_blk + p)),
            pl.BlockSpec((1, S, LANES), lambda p, b: (b, 0, v_blk + p)),
            pl.BlockSpec((wh, BQ, S), (lambda p, b: (p, 0, 0)) if per_head_w else (lambda p, b: (0, 0, 0))),
        ],
        out_specs=pl.BlockSpec((1, S, LANES), lambda p, b: (b, 0, p)),
        out_shape=jax.ShapeDtypeStruct((B, S, n_pairs * LANES), BF16),
        compiler_params=_cparams("parallel", "parallel"),
        name=name,
    )(q_arr, k_arr, v_arr, w)


I16_MIN = -32768


def _count16(mask):
    ones = jnp.where(mask, jnp.asarray(1, BF16), jnp.asarray(0, BF16))
    acc = ones[:, :LANES]
    for j in range(1, ones.shape[1] // LANES):
        acc = acc + ones[:, j * LANES:(j + 1) * LANES]
    return jnp.sum(acc.astype(F32), axis=1, keepdims=True)


def _count(mask):
    return jnp.sum(jnp.where(mask, 1.0, 0.0).astype(F32), axis=1, keepdims=True)


def _dsa_kernel(sq_ref, sk_ref, sv_ref, iq_ref, ik_ref, iw_ref, w_ref, o_ref, *scratch, S, n_sel):
    lo, hi = _head_masks()
    idx_bits = (S - 1).bit_length()
    ksel = float(n_sel)
    nq = S // BQ
    key_refs, half_refs = scratch[:nq], scratch[nq:]
    for i in range(nq):
        n = (i + 1) * BQ
        rows = slice(i * BQ, (i + 1) * BQ)
        ikn = ik_ref[0, :n, :]
        iwb = iw_ref[0, rows, :]
        score = jnp.zeros((BQ, n), F32)
        for hh in range(IDX_HEADS):
            c = hh // 2
            iqc = jnp.where(lo if hh % 2 == 0 else hi, iq_ref[0, rows, c * LANES:(c + 1) * LANES],
                            jnp.zeros((), BF16))
            score = score + iwb[:, hh:hh + 1] * jnp.maximum(_nt_dot(iqc, ikn), 0.0)
        col = lax.broadcasted_iota(jnp.int32, (BQ, n), 1)
        row = lax.broadcasted_iota(jnp.int32, (BQ, n), 0)
        bits = lax.bitcast_convert_type(score, jnp.int32)
        key = bits ^ ((bits >> 31) & jnp.int32(0x7FFFFFFF))
        key = jnp.where(col <= row + i * BQ, key, jnp.int32(INT_MIN))
        key_refs[i][...] = key
        half_refs[i][...] = (key >> 16).astype(jnp.int16)

    def bisect16(base_counts):
        def ok(i, cand):
            return base_counts[i] + _count16(half_refs[i][...] >= cand.astype(jnp.int16)) >= ksel

        zero = jnp.zeros((BQ, 1), jnp.int32)
        t0 = tuple(jnp.where(ok(i, zero), jnp.int32(0), jnp.int32(I16_MIN)) for i in range(nq))

        def step(it, ts):
            bit = jnp.left_shift(jnp.int32(1), 14 - it)
            return tuple(jnp.where(ok(i, ts[i] | bit), ts[i] | bit, ts[i]) for i in range(nq))

        return lax.fori_loop(0, 15, step, t0)

    tau_hi = bisect16(tuple(jnp.zeros((BQ, 1), F32) for _ in range(nq)))
    above = []
    for i in range(nq):
        hi16 = half_refs[i][...]
        th = tau_hi[i].astype(jnp.int16)
        above.append(_count16(hi16 > th))
        lo16 = ((key_refs[i][...] & jnp.int32(0xFFFF)) - jnp.int32(32768)).astype(jnp.int16)
        half_refs[i][...] = jnp.where(hi16 == th, lo16, jnp.asarray(I16_MIN, jnp.int16))
    tau_lo = bisect16(tuple(above))

    for i in range(nq):
        n = (i + 1) * BQ
        rows = slice(i * BQ, (i + 1) * BQ)
        key = key_refs[i][...]
        tau = tau_hi[i] * jnp.int32(65536) + (tau_lo[i] + jnp.int32(32768))
        col = lax.broadcasted_iota(jnp.int32, (BQ, n), 1)
        gt = key > tau
        eq = key == tau
        need = ksel - _count(gt)
        excess = (_count(eq) > need) & (tau > jnp.int32(INT_MIN))
        any_excess = jnp.max(jnp.where(excess, 1, 0).astype(jnp.int32)) > 0

        def idx_step(it, j, eq=eq, col=col, need=need):
            cand = j + jnp.left_shift(jnp.int32(1), idx_bits - 1 - it)
            return jnp.where(_count(eq & (col < cand)) < need, cand, j)

        jmax = lax.fori_loop(0, jnp.where(any_excess, idx_bits, 0), idx_step, jnp.zeros((BQ, 1), jnp.int32))
        jmax = jnp.where(excess, jmax, jnp.int32(S))
        madd = jnp.where(gt | (eq & (col <= jmax)), 0.0, NEG).astype(F32)

        for p in range(N_HEADS_DSA // 2):
            q2 = sq_ref[0, rows, p * LANES:(p + 1) * LANES]
            k2 = sk_ref[0, :n, p * LANES:(p + 1) * LANES]
            v2 = sv_ref[0, :n, p * LANES:(p + 1) * LANES]
            outs = []
            for h in range(2):
                qh = _select_head(q2, lo if h == 0 else hi)
                s = _nt_dot(qh, k2) + w_ref[2 * p + h, :, S - n:] + madd
                outs.append(_softmax_pv(s, v2))
            o_ref[0, rows, p * LANES:(p + 1) * LANES] = jnp.where(lo, outs[0], outs[1]).astype(o_ref.dtype)


def _dsa(ua3, ub3, w):
    B, S, _ = ua3.shape
    n_sel = min(DSA_TOPK, S // 4)
    assert n_sel <= BQ
    hw = N_HEADS_DSA * HEAD_DIM
    return pl.pallas_call(
        functools.partial(_dsa_kernel, S=S, n_sel=n_sel),
        grid=(B,),
        in_specs=[
            pl.BlockSpec((1, S, hw), lambda b: (b, 0, SQ_BLK * LANES // hw)),
            pl.BlockSpec((1, S, hw), lambda b: (b, 0, SK_BLK * LANES // hw)),
            pl.BlockSpec((1, S, hw), lambda b: (b, 0, SV_BLK * LANES // hw)),
            pl.BlockSpec((1, S, IDX_HEADS * IDX_DIM), lambda b: (b, 0, IQ_BLK * LANES // (IDX_HEADS * IDX_DIM))),
            pl.BlockSpec((1, S, LANES), lambda b: (b, 0, IK_BLK)),
            pl.BlockSpec((1, S, LANES), lambda b: (b, 0, (MLA_Q_RANK + MLA_KV_RANK) // LANES)),
            pl.BlockSpec((N_HEADS_DSA, BQ, S), lambda b: (0, 0, 0), pipeline_mode=pl.Buffered(1)),
        ],
        out_specs=pl.BlockSpec((1, S, hw), lambda b: (b, 0, 0)),
        out_shape=jax.ShapeDtypeStruct((B, S, hw), BF16),
        scratch_shapes=[pltpu.VMEM((BQ, (i + 1) * BQ), dt) for dt in (jnp.int32, jnp.int16)
                        for i in range(S // BQ)],
        compiler_params=_cparams("parallel"),
        name="dsa_attn",
    )(ua3, ua3, ua3, ua3, ua3, ub3, w)


def _mla_prep_kernel(ub_ref, qn_ref, kvn_ref, wq_ref, wk_ref, wv_ref, c_ref, s1_ref, s2_ref,
                     q_ref, k_ref, v_ref, *, scale):
    ub = ub_ref[...]
    cq = ub[:, :MLA_Q_RANK]
    ckv = ub[:, MLA_Q_RANK:MLA_Q_RANK + MLA_KV_RANK]
    kr_chunk = ub[:, MLA_Q_RANK + MLA_KV_RANK:]

    def rms(t, g):
        return t * lax.rsqrt(jnp.mean(t * t, axis=-1, keepdims=True) + RMS_EPS) * g

    qn = rms(cq, qn_ref[...]).astype(BF16)
    kvn = rms(ckv, kvn_ref[...]).astype(BF16)
    q = jnp.dot(qn, wq_ref[...], preferred_element_type=F32)
    kn = jnp.dot(kvn, wk_ref[...], preferred_element_type=F32)
    v = jnp.dot(kvn, wv_ref[...], preferred_element_type=F32)
    c, s1, s2 = c_ref[...], s1_ref[...], s2_ref[...]

    def rope(t):
        return t * c + pltpu.roll(t, 16, 1) * s1 + pltpu.roll(t, LANES - 16, 1) * s2

    lane = lax.broadcasted_iota(jnp.int32, (1, LANES), 1)
    kr = jnp.where((lane >= MLA_NOPE) & (lane < MLA_NOPE + MLA_ROPE), rope(kr_chunk), 0.0)
    for h in range(N_HEADS_MLA):
        cols = slice(h * LANES, (h + 1) * LANES)
        q_ref[:, cols] = (rope(q[:, cols]) * scale).astype(BF16)
        k_ref[:, cols] = (kn[:, cols] + kr).astype(BF16)
    v_ref[...] = v.astype(BF16)


def _mla_prep(ub, qn, kvn, wq, wk, wv, c, s1, s2, tm=512):
    T = ub.shape[0]
    scale = (MLA_NOPE + MLA_ROPE) ** -0.5
    row = lambda i: (i, 0)
    fix = lambda i: (0, 0)
    hq = N_HEADS_MLA * LANES
    hv = N_HEADS_MLA * MLA_V
    return pl.pallas_call(
        functools.partial(_mla_prep_kernel, scale=scale),
        grid=(T // tm,),
        in_specs=[
            pl.BlockSpec((tm, WB_COLS), row),
            pl.BlockSpec((1, MLA_Q_RANK), fix), pl.BlockSpec((1, MLA_KV_RANK), fix),
            pl.BlockSpec((MLA_Q_RANK, hq), fix), pl.BlockSpec((MLA_KV_RANK, hq), fix),
            pl.BlockSpec((MLA_KV_RANK, hv), fix),
            pl.BlockSpec((tm, LANES), row), pl.BlockSpec((tm, LANES), row), pl.BlockSpec((tm, LANES), row),
        ],
        out_specs=[pl.BlockSpec((tm, hq), row), pl.BlockSpec((tm, hq), row), pl.BlockSpec((tm, hv), row)],
        out_shape=[jax.ShapeDtypeStruct((T, hq), BF16), jax.ShapeDtypeStruct((T, hq), BF16),
                   jax.ShapeDtypeStruct((T, hv), BF16)],
        compiler_params=_cparams("parallel"),
        name="mla_prep",
    )(ub, qn, kvn, wq, wk, wv, c, s1, s2)


def _layer_norm(y, g, b):
    mu = jnp.mean(y, axis=-1, keepdims=True)
    yc = y - mu
    var = jnp.mean(yc * yc, axis=-1, keepdims=True)
    return yc * lax.rsqrt(var + LN_EPS) * g + b


def _outproj_kernel(oa_ref, ob_ref, oc_ref, wa_ref, wb_ref, wc_ref, x_ref, g_ref, b_ref, o_ref, *, alpha):
    acc = jnp.dot(oa_ref[...], wa_ref[...], preferred_element_type=F32)
    acc += jnp.dot(ob_ref[...], wb_ref[...], preferred_element_type=F32)
    acc += jnp.dot(oc_ref[...], wc_ref[...], preferred_element_type=F32)
    o_ref[...] = _layer_norm(alpha * x_ref[...] + acc, g_ref[...], b_ref[...])


def _outproj_ln(oa, ob, oc, wa, wb, wc, x, g, b, alpha, tm=512):
    T, D = x.shape
    row = lambda i: (i, 0)
    fix = lambda i: (0, 0)
    return pl.pallas_call(
        functools.partial(_outproj_kernel, alpha=alpha),
        grid=(T // tm,),
        in_specs=[pl.BlockSpec((tm, oa.shape[1]), row), pl.BlockSpec((tm, ob.shape[1]), row),
                  pl.BlockSpec((tm, oc.shape[1]), row),
                  pl.BlockSpec(wa.shape, fix), pl.BlockSpec(wb.shape, fix), pl.BlockSpec(wc.shape, fix),
                  pl.BlockSpec((tm, D), row), pl.BlockSpec((1, D), fix), pl.BlockSpec((1, D), fix)],
        out_specs=pl.BlockSpec((tm, D), row),
        out_shape=jax.ShapeDtypeStruct((T, D), F32),
        compiler_params=_cparams("parallel"),
        name="out_proj_ln",
    )(oa, ob, oc, wa, wb, wc, x, g, b)


MOE_TT = 256
MOE_CH = 8
MOE_BM = 1024
MOE_SLOTS = -(-(TOP_K * MOE_TT + N_EXPERTS * (MOE_CH - 1)) // MOE_TT) * MOE_TT
MOE_MAXCH = MOE_SLOTS // MOE_CH
MOE_RUNS = (4, 2, 1)
MOE_RUN_CAPS = (MOE_MAXCH // 4, N_EXPERTS, N_EXPERTS)
MOE_NRUN = sum(MOE_RUN_CAPS)
MOE_ZROWS = 256
HALF = D_MODEL // 2
HI_MASK = -65536


def _pack_bf16_pairs(t):
    lo = lax.bitcast_convert_type(t[:, :HALF], jnp.int32)
    hi = lax.bitcast_convert_type(t[:, HALF:], jnp.int32)
    return (hi & jnp.int32(HI_MASK)) | lax.shift_right_logical(lo, jnp.int32(16))


def _unpack_bf16_pairs(w):
    lo = lax.bitcast_convert_type(lax.shift_left(w, jnp.int32(16)), F32)
    hi = lax.bitcast_convert_type(w & jnp.int32(HI_MASK), F32)
    return jnp.concatenate([lo, hi], axis=1).astype(BF16)


def _route_kernel(x_ref, rw_ref, rb_ref, slot_ref, gate_ref, slott_ref, meta_ref):
    tt = x_ref.shape[0]
    logits = jnp.dot(x_ref[...].astype(BF16), rw_ref[...], preferred_element_type=F32)
    scores = 1.0 / (1.0 + jnp.exp(-logits))
    lane = lax.broadcasted_iota(jnp.int32, logits.shape, 1)
    lane_f = lane.astype(F32)
    work = jnp.where(lane < N_EXPERTS, scores + rb_ref[...], NEG)
    sel = lane < 0
    firsts = []
    for _ in range(TOP_K):
        m = jnp.max(work, axis=1, keepdims=True)
        first = jnp.min(jnp.where(work == m, lane_f, float(LANES)), axis=1, keepdims=True)
        pick = lane_f == first
        sel = sel | pick
        work = jnp.where(pick, NEG, work)
        firsts.append(first)
    gs = jnp.where(sel, scores, 0.0)
    gates = gs / jnp.sum(gs, axis=1, keepdims=True) * ROUTED_SCALE

    sel_f = jnp.where(sel, 1.0, 0.0).astype(F32)
    r_i = lax.broadcasted_iota(jnp.int32, (tt, tt), 0)
    c_i = lax.broadcasted_iota(jnp.int32, (tt, tt), 1)
    below = jnp.where(r_i > c_i, 1.0, 0.0).astype(BF16)
    rank = jnp.dot(below, sel_f.astype(BF16), preferred_element_type=F32)
    cnt = jnp.sum(sel_f, axis=0, keepdims=True)
    cnt_pad = jnp.floor((cnt + (MOE_CH - 1)) * (1.0 / MOE_CH)) * MOE_CH
    e_r = lax.broadcasted_iota(jnp.int32, (LANES, LANES), 0)
    e_c = lax.broadcasted_iota(jnp.int32, (LANES, LANES), 1)
    before = jnp.where(e_r < e_c, 1.0, 0.0).astype(BF16)
    base = jnp.dot(jnp.broadcast_to(cnt_pad, (8, LANES)).astype(BF16), before,
                   preferred_element_type=F32)[0:1]
    slot = base + rank

    slot_tk = jnp.full(logits.shape, -1.0, F32)
    gate_tk = jnp.zeros(logits.shape, F32)
    for k, first in enumerate(firsts):
        pick = lane_f == first
        slot_k = jnp.sum(jnp.where(pick, slot, 0.0), axis=1, keepdims=True)
        gate_k = jnp.sum(jnp.where(pick, gates, 0.0), axis=1, keepdims=True)
        slot_tk = jnp.where(lane == k, slot_k, slot_tk)
        gate_tk = jnp.where(lane == k, gate_k, gate_tk)
    slot_ref[...] = slot_tk.astype(jnp.int32)
    gate_ref[...] = gate_tk
    slott_ref[0] = slot_tk.T[:TOP_K].astype(jnp.int32)
    row = lax.broadcasted_iota(jnp.int32, (8, LANES), 0)
    meta = jnp.where(row == 0, jnp.broadcast_to(cnt_pad, (8, LANES)), jnp.broadcast_to(base, (8, LANES)))
    meta_ref[0] = meta.astype(jnp.int32)


def _route(x, rw, rb):
    T, D = x.shape
    nt = T // MOE_TT
    return pl.pallas_call(
        _route_kernel,
        grid=(nt,),
        in_specs=[pl.BlockSpec((MOE_TT, D), lambda i: (i, 0)), pl.BlockSpec((D, LANES), lambda i: (0, 0)),
                  pl.BlockSpec((1, LANES), lambda i: (0, 0))],
        out_specs=[pl.BlockSpec((MOE_TT, LANES), lambda i: (i, 0)), pl.BlockSpec((MOE_TT, LANES), lambda i: (i, 0)),
                   pl.BlockSpec((1, TOP_K, MOE_TT), lambda i: (i, 0, 0)), pl.BlockSpec((1, 8, LANES), lambda i: (i, 0, 0))],
        out_shape=[jax.ShapeDtypeStruct((T, LANES), jnp.int32), jax.ShapeDtypeStruct((T, LANES), F32),
                   jax.ShapeDtypeStruct((nt, TOP_K, MOE_TT), jnp.int32), jax.ShapeDtypeStruct((nt, 8, LANES), jnp.int32)],
        compiler_params=_cparams("parallel"),
        name="moe_route",
    )(x, rw, rb)


def _chunk_copy(src_ref, dst_ref, sem):
    return pltpu.make_async_copy(src_ref, dst_ref, sem)


def _wait_n(n, wait_one):
    def eight(_, c):
        for _ in range(8):
            wait_one()
        return c

    def one(_, c):
        wait_one()
        return c
    lax.fori_loop(0, lax.shift_right_logical(n, 3), eight, 0)
    lax.fori_loop(0, n & 7, one, 0)


def _for_each_run(t, src_ref, dst_ref, num_ref, fn):
    off = 0
    for ci, (run, cap) in enumerate(zip(MOE_RUNS, MOE_RUN_CAPS)):
        def body(p, c, off=off, rows=run * MOE_CH):
            k = t * MOE_NRUN + off + p
            fn(rows, pl.multiple_of(src_ref[k], MOE_CH), pl.multiple_of(dst_ref[k], MOE_CH))
            return c
        lax.fori_loop(0, num_ref[t * len(MOE_RUNS) + ci], body, 0)
        off += cap


def _wait_runs(t, num_ref, wait_one):
    for ci, run in enumerate(MOE_RUNS):
        _wait_n(num_ref[t * len(MOE_RUNS) + ci], functools.partial(wait_one, run * MOE_CH))


def _dispatch_kernel(src_ref, dst_ref, num_ref, tail_start_ref, tail_n_ref,
                     x_ref, slott_ref, xs_ref, loc_ref, zero_ref, sem, tail_sem):
    i = pl.program_id(0)
    nt = pl.num_programs(0)
    cur = i % 2
    s_iota = lax.broadcasted_iota(jnp.int32, (MOE_SLOTS, MOE_TT), 0)
    hit = s_iota == slott_ref[0, 0:1, :]
    for k in range(1, TOP_K):
        hit = hit | (s_iota == slott_ref[0, k:k + 1, :])
    onehot = jnp.where(hit, 1.0, 0.0).astype(BF16)
    rows = jnp.dot(onehot, x_ref[...].astype(BF16), preferred_element_type=F32)
    loc_ref[cur] = _pack_bf16_pairs(rows)

    def run_copy(buf, rows, local, glob):
        return _chunk_copy(loc_ref.at[buf, pl.ds(local, rows), :], xs_ref.at[pl.ds(glob, rows), :], sem)

    @pl.when(i > 0)
    def _():
        _wait_runs(i - 1, num_ref, lambda rows: run_copy(0, rows, 0, 0).wait())

    _for_each_run(i, src_ref, dst_ref, num_ref, lambda rows, local, glob: run_copy(cur, rows, local, glob).start())

    @pl.when(i == nt - 1)
    def _():
        zero_ref[...] = jnp.zeros_like(zero_ref)

        def fill(rows_per_copy, first, last):
            z = zero_ref.at[pl.ds(0, rows_per_copy), :]

            def region(e, c):
                g = tail_start_ref[e]

                def start(j, c2):
                    row = pl.multiple_of(g + j * rows_per_copy, MOE_CH)
                    _chunk_copy(z, xs_ref.at[pl.ds(row, rows_per_copy), :], tail_sem).start()
                    return c2
                return lax.fori_loop(0, tail_n_ref[e], start, c)
            lax.fori_loop(first, last, region, 0)

            def region_wait(e, c):
                _wait_n(tail_n_ref[e], lambda: _chunk_copy(z, xs_ref.at[pl.ds(0, rows_per_copy), :],
                                                           tail_sem).wait())
                return c
            lax.fori_loop(first, last, region_wait, 0)

        fill(MOE_CH, 0, N_EXPERTS)
        fill(MOE_ZROWS, N_EXPERTS, N_EXPERTS + 1)
        _wait_runs(i, num_ref, lambda rows: run_copy(0, rows, 0, 0).wait())


def _dispatch(x, slot_kt, tables, n_rows):
    T, D = x.shape
    nt = T // MOE_TT
    src, dst, num, tail_start, tail_n = tables
    return pl.pallas_call(
        _dispatch_kernel,
        grid_spec=pltpu.PrefetchScalarGridSpec(
            num_scalar_prefetch=5,
            grid=(nt,),
            in_specs=[pl.BlockSpec((MOE_TT, D), lambda i, *_: (i, 0)),
                      pl.BlockSpec((1, TOP_K, MOE_TT), lambda i, *_: (i, 0, 0))],
            out_specs=pl.BlockSpec(memory_space=pl.ANY),
            scratch_shapes=[pltpu.VMEM((2, MOE_SLOTS, HALF), jnp.int32), pltpu.VMEM((MOE_ZROWS, HALF), jnp.int32),
                            pltpu.SemaphoreType.DMA(()), pltpu.SemaphoreType.DMA(())],
        ),
        out_shape=jax.ShapeDtypeStruct((n_rows, HALF), jnp.int32),
        compiler_params=_cparams("arbitrary"),
        name="moe_dispatch",
    )(src, dst, num, tail_start, tail_n, x, slot_kt)


def _expert_kernel(be_ref, nu_ref, xs_ref, w1_ref, w3_ref, w2_ref, ys_ref, w1b_ref, w3b_ref, w2b_ref):
    i = pl.program_id(0)

    @pl.when((i == 0) | (be_ref[i] != be_ref[jnp.maximum(i - 1, 0)]))
    def _():
        w1b_ref[...] = w1_ref[0].astype(BF16)
        w3b_ref[...] = w3_ref[0].astype(BF16)
        w2b_ref[...] = w2_ref[0].astype(BF16)

    @pl.when(i < nu_ref[0])
    def _():
        xb = _unpack_bf16_pairs(xs_ref[...])
        h1 = jnp.dot(xb, w1b_ref[...], preferred_element_type=F32)
        h3 = jnp.dot(xb, w3b_ref[...], preferred_element_type=F32)
        h = h1 / (1.0 + jnp.exp(-h1)) * h3
        y = jnp.dot(h.astype(BF16), w2b_ref[...], preferred_element_type=F32)
        ys_ref[...] = _pack_bf16_pairs(y.astype(BF16).astype(F32))

    @pl.when(i >= nu_ref[0])
    def _():
        ys_ref[...] = jnp.zeros_like(ys_ref)


def _experts(xs, block_expert, n_used, w1, w3, w2):
    n_rows = xs.shape[0]
    nb = n_rows // MOE_BM
    _, D, ff = w1.shape
    blk = lambda i, be, nu: (jnp.minimum(i, nu[0] - 1), 0)
    return pl.pallas_call(
        _expert_kernel,
        grid_spec=pltpu.PrefetchScalarGridSpec(
            num_scalar_prefetch=2,
            grid=(nb,),
            in_specs=[pl.BlockSpec((MOE_BM, HALF), blk),
                      pl.BlockSpec((1, D, ff), lambda i, be, nu: (be[i], 0, 0)),
                      pl.BlockSpec((1, D, ff), lambda i, be, nu: (be[i], 0, 0)),
                      pl.BlockSpec((1, ff, D), lambda i, be, nu: (be[i], 0, 0))],
            out_specs=pl.BlockSpec((MOE_BM, HALF), lambda i, be, nu: (i, 0)),
            scratch_shapes=[pltpu.VMEM((D, ff), BF16), pltpu.VMEM((D, ff), BF16), pltpu.VMEM((ff, D), BF16)],
        ),
        out_shape=jax.ShapeDtypeStruct((n_rows, HALF), jnp.int32),
        compiler_params=_cparams("arbitrary"),
        name="moe_experts",
    )(block_expert, n_used, xs, w1, w3, w2)


def _combine_kernel(src_ref, dst_ref, num_ref, slot_ref, gate_ref, ys_ref, o_ref, loc_ref, sem):
    i = pl.program_id(0)
    nt = pl.num_programs(0)
    cur = i % 2

    def run_copy(buf, rows, local, glob):
        return _chunk_copy(ys_ref.at[pl.ds(glob, rows), :], loc_ref.at[buf, pl.ds(local, rows), :], sem.at[buf])

    def fetch(t, buf):
        _for_each_run(t, src_ref, dst_ref, num_ref, lambda rows, local, glob: run_copy(buf, rows, local, glob).start())

    @pl.when(i == 0)
    def _():
        loc_ref[...] = jnp.zeros_like(loc_ref)
        fetch(0, 0)

    @pl.when(i + 1 < nt)
    def _():
        fetch(i + 1, 1 - cur)

    _wait_runs(i, num_ref, lambda rows: run_copy(cur, rows, 0, 0).wait())

    s_iota = lax.broadcasted_iota(jnp.int32, (MOE_TT, MOE_SLOTS), 1)
    slots = slot_ref[...]
    gates = gate_ref[...]
    wgt = jnp.zeros((MOE_TT, MOE_SLOTS), F32)
    for k in range(TOP_K):
        wgt = jnp.where(s_iota == slots[:, k:k + 1], gates[:, k:k + 1], wgt)
    w_hi = wgt.astype(BF16)
    w_lo = (wgt - w_hi.astype(F32)).astype(BF16)
    ys = _unpack_bf16_pairs(loc_ref[cur])
    o_ref[...] = (jnp.dot(w_hi, ys, preferred_element_type=F32) + jnp.dot(w_lo, ys, preferred_element_type=F32))


def _combine(ys, slot_tk, gate_tk, tables):
    T = slot_tk.shape[0]
    nt = T // MOE_TT
    src, dst, num = tables
    return pl.pallas_call(
        _combine_kernel,
        grid_spec=pltpu.PrefetchScalarGridSpec(
            num_scalar_prefetch=3,
            grid=(nt,),
            in_specs=[pl.BlockSpec((MOE_TT, LANES), lambda i, *_: (i, 0)),
                      pl.BlockSpec((MOE_TT, LANES), lambda i, *_: (i, 0)),
                      pl.BlockSpec(memory_space=pl.ANY)],
            out_specs=pl.BlockSpec((MOE_TT, D_MODEL), lambda i, *_: (i, 0)),
            scratch_shapes=[pltpu.VMEM((2, MOE_SLOTS, HALF), jnp.int32), pltpu.SemaphoreType.DMA((2,))],
        ),
        out_shape=jax.ShapeDtypeStruct((T, D_MODEL), F32),
        compiler_params=_cparams("arbitrary"),
        name="moe_combine",
    )(src, dst, num, slot_tk, gate_tk, ys)


def _shared_ln_kernel(x_ref, r_ref, w1_ref, w3_ref, w2_ref, g_ref, b_ref, o_ref, *, alpha):
    x = x_ref[...]
    xb = x.astype(BF16)
    h1 = jnp.dot(xb, w1_ref[...], preferred_element_type=F32)
    h3 = jnp.dot(xb, w3_ref[...], preferred_element_type=F32)
    h = h1 / (1.0 + jnp.exp(-h1)) * h3
    y = jnp.dot(h.astype(BF16), w2_ref[...], preferred_element_type=F32)
    o_ref[...] = _layer_norm(alpha * x + (r_ref[...] + y), g_ref[...], b_ref[...])


def _shared_ln(x, routed, w1, w3, w2, g, b, alpha, tm=512):
    T, D = x.shape
    row = lambda i: (i, 0)
    fix = lambda i: (0, 0)
    return pl.pallas_call(
        functools.partial(_shared_ln_kernel, alpha=alpha),
        grid=(T // tm,),
        in_specs=[pl.BlockSpec((tm, D), row), pl.BlockSpec((tm, D), row), pl.BlockSpec(w1.shape, fix),
                  pl.BlockSpec(w3.shape, fix), pl.BlockSpec(w2.shape, fix), pl.BlockSpec((1, D), fix),
                  pl.BlockSpec((1, D), fix)],
        out_specs=pl.BlockSpec((tm, D), row),
        out_shape=jax.ShapeDtypeStruct((T, D), F32),
        compiler_params=_cparams("parallel"),
        name="shared_ffn_ln",
    )(x, routed, w1, w3, w2, g, b)


def _dispatch_tables(meta, n_blocks):
    cnt = meta[:, 0, :N_EXPERTS]
    base = meta[:, 1, :N_EXPERTS]
    tot = jnp.sum(cnt, axis=0)
    nblk = (tot + MOE_BM - 1) // MOE_BM
    bend = jnp.cumsum(nblk)
    pstart = (bend - nblk) * MOE_BM
    gstart = pstart[None, :] + jnp.cumsum(cnt, axis=0) - cnt
    n_used = bend[-1:]
    blocks = jnp.arange(n_blocks, dtype=jnp.int32)
    be = jnp.sum((jnp.minimum(blocks, n_used - 1)[:, None] >= bend[None, :]).astype(jnp.int32), axis=1)
    chunks = cnt // MOE_CH
    done = jnp.zeros_like(chunks)
    src_l, dst_l, num_l = [], [], []
    for run, cap in zip(MOE_RUNS, MOE_RUN_CAPS):
        m = (chunks - done) // run
        cum = jnp.cumsum(m, axis=1) - m
        p = jnp.arange(cap, dtype=jnp.int32)
        owned = cum[:, None, :] <= p[None, :, None]
        last = owned & ~jnp.concatenate([owned[:, :, 1:], jnp.zeros_like(owned[:, :, :1])], axis=-1)
        pick = lambda t: jnp.sum(jnp.where(last, t[:, None, :], 0), axis=-1)
        within = (p[None, :] - pick(cum)) * (run * MOE_CH) + pick(done) * MOE_CH
        src_l.append(pick(base) + within)
        dst_l.append(pick(gstart) + within)
        num_l.append(jnp.sum(m, axis=1))
        done = done + m * run
    i32 = lambda a: a.astype(jnp.int32)
    common = (i32(jnp.concatenate(src_l, axis=1).reshape(-1)), i32(jnp.concatenate(dst_l, axis=1).reshape(-1)),
              i32(jnp.stack(num_l, axis=1).reshape(-1)))
    tails = (i32(jnp.concatenate([pstart + tot, n_used * MOE_BM])),
             i32(jnp.concatenate([(nblk * MOE_BM - tot) // MOE_CH, (n_blocks - n_used) * (MOE_BM // MOE_ZROWS)])))
    return common, tails, i32(jnp.minimum(be, N_EXPERTS - 1)), i32(n_used)


def _moe_block(x1, rw, rb, w1, w3, w2, sw1, sw3, sw2, lg, lb, alpha):
    T = x1.shape[0]
    nt = T // MOE_TT
    n_blocks = -(-(TOP_K * T + nt * N_EXPERTS * (MOE_CH - 1)) // MOE_BM) + N_EXPERTS
    slot_tk, gate_tk, slot_kt, meta = _route(x1, rw, rb)
    common, tails, be, n_used = _dispatch_tables(meta, n_blocks)
    xs = _dispatch(x1, slot_kt, common + tails, n_blocks * MOE_BM)
    ys = _experts(xs, be, n_used, w1, w3, w2)
    routed = _combine(ys, slot_tk, gate_tk, common)
    return _shared_ln(x1, routed, sw1, sw3, sw2, lg, lb, alpha)


def _t5_bucket(dist):
    max_exact = N_BUCKETS // 2
    d_f = jnp.maximum(dist, 1).astype(F32)
    large = max_exact + (jnp.log(d_f / max_exact) / math.log(MAX_DISTANCE / max_exact)
                         * (N_BUCKETS - max_exact)).astype(jnp.int32)
    large = jnp.minimum(large, N_BUCKETS - 1)
    return jnp.where(dist < max_exact, dist, large)


def _bias_tables(rel_bias, S):
    dist = np.arange(S)
    e = rel_bias[_t5_bucket(jnp.asarray(dist, jnp.int32))].T
    mult = sum(((dist % d == 0) & (dist <= w)).astype(np.float32) for w, d in DILATIONS)
    e_dil = jnp.where(mult > 0, e[:N_HEADS_DIL] + jnp.log(jnp.maximum(mult, 1.0)), NEG)
    e_all = jnp.concatenate([e_dil, e[N_HEADS_DIL:], jnp.zeros((1, S), F32)], 0)
    period = S + BQ
    n_h = e_all.shape[0]
    a = jnp.concatenate([e_all[:, :S - BQ + 1][:, ::-1], jnp.full((n_h, BQ), NEG, F32),
                         e_all[:, S - BQ + 1:][:, ::-1]], 1)
    w = jnp.tile(a, (1, BQ))[:, :BQ * (period - 1)].reshape(n_h, BQ, period - 1)[:, :, :S].astype(F32)
    return w[:N_HEADS_DIL], w[N_HEADS_DIL:N_HEADS_DIL + N_HEADS_DSA], w[-1:]


def _rope_tables(positions):
    inv = ROPE_THETA ** (-jnp.arange(0, MLA_ROPE, 2, dtype=F32) / MLA_ROPE)
    ang = positions.astype(F32).reshape(-1, 1) * inv
    cos, sin = jnp.cos(ang), jnp.sin(ang)
    T, half = cos.shape
    one = jnp.ones((T, MLA_NOPE), F32)
    z = lambda n: jnp.zeros((T, n), F32)
    tail = LANES - MLA_NOPE - MLA_ROPE
    c = jnp.concatenate([one, cos, cos, jnp.ones((T, tail), F32)], 1)
    s1 = jnp.concatenate([z(MLA_NOPE + half), sin, z(tail)], 1)
    s2 = jnp.concatenate([z(MLA_NOPE), -sin, z(half + tail)], 1)
    return c, s1, s2


def _split_w_in(w):
    o = np.cumsum([0, 384, 384, 384, 256, 256, 256, 512, 64, 8, 256, 128, 32])
    ik = w[:, o[7]:o[8]]
    wa = jnp.concatenate([w[:, o[6]:o[7]], w[:, o[3]:o[6]], ik, ik, w[:, :o[3]]], 1)
    z = lambda n: jnp.zeros((w.shape[0], n), w.dtype)
    chunk = jnp.concatenate([w[:, o[8]:o[9]], z(MLA_NOPE - IDX_HEADS), w[:, o[11]:o[12]],
                             z(LANES - MLA_NOPE - MLA_ROPE)], 1)
    wb = jnp.concatenate([w[:, o[9]:o[11]], chunk], 1)
    return wa.astype(BF16), wb.astype(BF16)


def _split_mla_weights(w_uq, w_ukv):
    r = w_uq.shape[0]
    wq = w_uq.reshape(r, N_HEADS_MLA, MLA_NOPE + MLA_ROPE)
    wq = jnp.pad(wq, ((0, 0), (0, 0), (0, LANES - MLA_NOPE - MLA_ROPE))).reshape(r, N_HEADS_MLA * LANES)
    r = w_ukv.shape[0]
    wkv = w_ukv.reshape(r, N_HEADS_MLA, MLA_NOPE + MLA_V)
    wk = jnp.pad(wkv[:, :, :MLA_NOPE], ((0, 0), (0, 0), (0, LANES - MLA_NOPE))).reshape(r, N_HEADS_MLA * LANES)
    wv = wkv[:, :, MLA_NOPE:].reshape(r, N_HEADS_MLA * MLA_V)
    return wq.astype(BF16), wk.astype(BF16), wv.astype(BF16)


def kernel(x, positions, w_in, mla_q_norm, mla_w_uq, mla_kv_norm, mla_w_ukv, w_out, ln1_g, ln1_b, router_w,
           router_bias, exp_w1, exp_w3, exp_w2, sh_w1, sh_w3, sh_w2, ln2_g, ln2_b, rel_bias):
    B, S, D = x.shape
    T = B * S
    depth = w_in.shape[0]
    alpha = (2 * depth) ** 0.25
    w_dil, w_dsa, w_causal = _bias_tables(rel_bias, S)
    rope_c, rope_s1, rope_s2 = _rope_tables(positions)
    na, nb = N_HEADS_DIL * HEAD_DIM, (N_HEADS_DIL + N_HEADS_DSA) * HEAD_DIM
    xf = x.reshape(T, D)
    for l in range(depth):
        wa, wb = _split_w_in(w_in[l])
        ua = _matmul(xf, wa, BF16)
        ub = _matmul(xf, wb, F32)
        ua3 = ua.reshape(B, S, WA_COLS)
        o_a = _attention(ua3, ua3, ua3, w_dil, q_blk=DQ_BLK, k_blk=DK_BLK, v_blk=DV_BLK,
                         n_pairs=N_HEADS_DIL // 2, packed=True, per_head_w=True, name="dilated_attn")
        o_b = _dsa(ua3, ub.reshape(B, S, WB_COLS), w_dsa)
        wq, wk, wv = _split_mla_weights(mla_w_uq[l], mla_w_ukv[l])
        q, k, v = _mla_prep(ub, mla_q_norm[l][None], mla_kv_norm[l][None], wq, wk, wv, rope_c, rope_s1, rope_s2)
        o_c = _attention(q.reshape(B, S, -1), k.reshape(B, S, -1), v.reshape(B, S, -1), w_causal,
                         q_blk=0, k_blk=0, v_blk=0, n_pairs=N_HEADS_MLA // 2, packed=False, per_head_w=False,
                         name="mla_attn")
        wo = w_out[l].astype(BF16)
        x1 = _outproj_ln(o_a.reshape(T, -1), o_b.reshape(T, -1), o_c.reshape(T, -1),
                         wo[:na], wo[na:nb], wo[nb:], xf, ln1_g[l][None], ln1_b[l][None], alpha)
        rw = jnp.pad(router_w[l], ((0, 0), (0, LANES - N_EXPERTS))).astype(BF16)
        rb = jnp.pad(router_bias[l].astype(F32), (0, LANES - N_EXPERTS))[None]
        xf = _moe_block(x1, rw, rb, exp_w1[l], exp_w3[l], exp_w2[l], sh_w1[l].astype(BF16), sh_w3[l].astype(BF16),
                        sh_w2[l].astype(BF16), ln2_g[l][None], ln2_b[l][None], alpha)
    return xf.reshape(B, S, D)
```

```python
import functools
import math

import jax
import jax.numpy as jnp
import numpy as np
from jax import lax
from jax.experimental import pallas as pl
from jax.experimental.pallas import tpu as pltpu

F32 = jnp.float32
BF16 = jnp.bfloat16

D_MODEL = 1024
HEAD_DIM = 64
N_HEADS_DIL = 6
N_HEADS_DSA = 4
N_HEADS_MLA = 6
DILATIONS = ((128, 1), (512, 4), (2048, 16))
IDX_HEADS = 8
IDX_DIM = 64
DSA_TOPK = 256
MLA_Q_RANK = 256
MLA_KV_RANK = 128
MLA_NOPE = 64
MLA_ROPE = 32
MLA_V = 64
ROPE_THETA = 10000.0
N_BUCKETS = 32
MAX_DISTANCE = 2048
N_EXPERTS = 64
TOP_K = 8
EXPERT_FF = 256
ROUTED_SCALE = 2.5
LN_EPS = 1e-5
RMS_EPS = 1e-6

LANES = 128
NEG = -1e30
INT_MIN = -2147483648
VMEM_LIMIT = 56 * 1024 * 1024
BQ = 256

WA_COLS = 3 * N_HEADS_DIL * HEAD_DIM + 3 * N_HEADS_DSA * HEAD_DIM + IDX_HEADS * IDX_DIM + 2 * IDX_DIM
WB_COLS = MLA_Q_RANK + MLA_KV_RANK + LANES
IQ_BLK = 0
SQ_BLK, SK_BLK, SV_BLK = 4, 6, 8
IK_BLK = 10
DQ_BLK, DK_BLK, DV_BLK = 11, 14, 17


def _cparams(*sem):
    return pltpu.CompilerParams(dimension_semantics=sem, vmem_limit_bytes=VMEM_LIMIT)


def _mm_kernel(x_ref, w_ref, o_ref):
    o_ref[...] = jnp.dot(x_ref[...].astype(BF16), w_ref[...], preferred_element_type=F32).astype(o_ref.dtype)


def _matmul(x, w, out_dtype, tm=512):
    T, K = x.shape
    N = w.shape[1]
    return pl.pallas_call(
        _mm_kernel,
        grid=(T // tm,),
        in_specs=[pl.BlockSpec((tm, K), lambda i: (i, 0)), pl.BlockSpec((K, N), lambda i: (0, 0))],
        out_specs=pl.BlockSpec((tm, N), lambda i: (i, 0)),
        out_shape=jax.ShapeDtypeStruct((T, N), out_dtype),
        compiler_params=_cparams("parallel"),
        name="in_proj",
    )(x, w)


def _head_masks():
    lane = lax.broadcasted_iota(jnp.int32, (1, LANES), 1)
    return lane < HEAD_DIM, lane >= HEAD_DIM


def _select_head(q2, head_lanes):
    return jnp.where(head_lanes, q2 * jnp.asarray(HEAD_DIM ** -0.5, BF16), jnp.zeros((), BF16))


def _softmax_pv(s, v):
    m = jnp.max(s, axis=1, keepdims=True)
    p = jnp.exp(s - m)
    l = jnp.sum(p, axis=1, keepdims=True)
    return jnp.dot(p.astype(BF16), v, preferred_element_type=F32) / l


def _nt_dot(a, b):
    return lax.dot_general(a, b, (((1,), (1,)), ((), ())), preferred_element_type=F32)


def _attn_kernel(q_ref, k_ref, v_ref, w_ref, o_ref, *, S, packed, per_head_w):
    lo, hi = _head_masks()
    for i in range(S // BQ):
        n = (i + 1) * BQ
        rows = slice(i * BQ, (i + 1) * BQ)
        v2 = v_ref[0, :n, :]
        outs = []
        for h in range(2):
            if packed:
                qh = _select_head(q_ref[0, rows, :], lo if h == 0 else hi)
                kh = k_ref[0, :n, :]
            else:
                qh = q_ref[0, rows, h * LANES:(h + 1) * LANES]
                kh = k_ref[0, :n, h * LANES:(h + 1) * LANES]
            s = _nt_dot(qh, kh) + w_ref[h if per_head_w else 0, :, S - n:]
            outs.append(_softmax_pv(s, v2))
        o_ref[0, rows, :] = jnp.where(lo, outs[0], outs[1]).astype(o_ref.dtype)


def _attention(q_arr, k_arr, v_arr, w, *, q_blk, k_blk, v_blk, n_pairs, packed, per_head_w, name):
    B, S, _ = q_arr.shape
    qk_w = LANES if packed else 2 * LANES
    wh = 2 if per_head_w else 1
    return pl.pallas_call(
        functools.partial(_attn_kernel, S=S, packed=packed, per_head_w=per_head_w),
        grid=(n_pairs, B),
        in_specs=[
            pl.BlockSpec((1, S, qk_w), lambda p, b: (b, 0, q_blk + p)),
            pl.BlockSpec((1, S, qk_w), lambda p, b: (b, 0, k_blk + p)),
            pl.BlockSpec((1, S, LANES), lambda p, b: (b, 0, v_blk + p)),
            pl.BlockSpec((wh, BQ, S), (lambda p, b: (p, 0, 0)) if per_head_w else (lambda p, b: (0, 0, 0))),
        ],
        out_specs=pl.BlockSpec((1, S, LANES), lambda p, b: (b, 0, p)),
        out_shape=jax.ShapeDtypeStruct((B, S, n_pairs * LANES), BF16),
        compiler_params=_cparams("parallel", "parallel"),
        name=name,
    )(q_arr, k_arr, v_arr, w)


I16_MIN = -32768


def _count16(mask):
    ones = jnp.where(mask, jnp.asarray(1, BF16), jnp.asarray(0, BF16))
    acc = ones[:, :LANES]
    for j in range(1, ones.shape[1] // LANES):
        acc = acc + ones[:, j * LANES:(j + 1) * LANES]
    return jnp.sum(acc.astype(F32), axis=1, keepdims=True)


def _count(mask):
    return jnp.sum(jnp.where(mask, 1.0, 0.0).astype(F32), axis=1, keepdims=True)


def _dsa_kernel(sq_ref, sk_ref, sv_ref, iq_ref, ik_ref, iw_ref, w_ref, o_ref, *scratch, S, n_sel):
    lo, hi = _head_masks()
    idx_bits = (S - 1).bit_length()
    ksel = float(n_sel)
    nq = S // BQ
    key_refs, half_refs = scratch[:nq], scratch[nq:]
    for i in range(nq):
        n = (i + 1) * BQ
        rows = slice(i * BQ, (i + 1) * BQ)
        ikn = ik_ref[0, :n, :]
        iwb = iw_ref[0, rows, :]
        score = jnp.zeros((BQ, n), F32)
        for hh in range(IDX_HEADS):
            c = hh // 2
            iqc = jnp.where(lo if hh % 2 == 0 else hi, iq_ref[0, rows, c * LANES:(c + 1) * LANES],
                            jnp.zeros((), BF16))
            score = score + iwb[:, hh:hh + 1] * jnp.maximum(_nt_dot(iqc, ikn), 0.0)
        col = lax.broadcasted_iota(jnp.int32, (BQ, n), 1)
        row = lax.broadcasted_iota(jnp.int32, (BQ, n), 0)
        bits = lax.bitcast_convert_type(score, jnp.int32)
        key = bits ^ ((bits >> 31) & jnp.int32(0x7FFFFFFF))
        key = jnp.where(col <= row + i * BQ, key, jnp.int32(INT_MIN))
        key_refs[i][...] = key
        half_refs[i][...] = (key >> 16).astype(jnp.int16)

    def bisect16(base_counts):
        def ok(i, cand):
            return base_counts[i] + _count16(half_refs[i][...] >= cand.astype(jnp.int16)) >= ksel

        zero = jnp.zeros((BQ, 1), jnp.int32)
        t0 = tuple(jnp.where(ok(i, zero), jnp.int32(0), jnp.int32(I16_MIN)) for i in range(nq))

        def step(it, ts):
            bit = jnp.left_shift(jnp.int32(1), 14 - it)
            return tuple(jnp.where(ok(i, ts[i] | bit), ts[i] | bit, ts[i]) for i in range(nq))

        return lax.fori_loop(0, 15, step, t0)

    tau_hi = bisect16(tuple(jnp.zeros((BQ, 1), F32) for _ in range(nq)))
    above = []
    for i in range(nq):
        hi16 = half_refs[i][...]
        th = tau_hi[i].astype(jnp.int16)
        above.append(_count16(hi16 > th))
        lo16 = ((key_refs[i][...] & jnp.int32(0xFFFF)) - jnp.int32(32768)).astype(jnp.int16)
        half_refs[i][...] = jnp.where(hi16 == th, lo16, jnp.asarray(I16_MIN, jnp.int16))
    tau_lo = bisect16(tuple(above))

    for i in range(nq):
        n = (i + 1) * BQ
        rows = slice(i * BQ, (i + 1) * BQ)
        key = key_refs[i][...]
        tau = tau_hi[i] * jnp.int32(65536) + (tau_lo[i] + jnp.int32(32768))
        col = lax.broadcasted_iota(jnp.int32, (BQ, n), 1)
        gt = key > tau
        eq = key == tau
        need = ksel - _count(gt)
        excess = (_count(eq) > need) & (tau > jnp.int32(INT_MIN))
        any_excess = jnp.max(jnp.where(excess, 1, 0).astype(jnp.int32)) > 0

        def idx_step(it, j, eq=eq, col=col, need=need):
            cand = j + jnp.left_shift(jnp.int32(1), idx_bits - 1 - it)
            return jnp.where(_count(eq & (col < cand)) < need, cand, j)

        jmax = lax.fori_loop(0, jnp.where(any_excess, idx_bits, 0), idx_step, jnp.zeros((BQ, 1), jnp.int32))
        jmax = jnp.where(excess, jmax, jnp.int32(S))
        madd = jnp.where(gt | (eq & (col <= jmax)), 0.0, NEG).astype(F32)

        for p in range(N_HEADS_DSA // 2):
            q2 = sq_ref[0, rows, p * LANES:(p + 1) * LANES]
            k2 = sk_ref[0, :n, p * LANES:(p + 1) * LANES]
            v2 = sv_ref[0, :n, p * LANES:(p + 1) * LANES]
            outs = []
            for h in range(2):
                qh = _select_head(q2, lo if h == 0 else hi)
                s = _nt_dot(qh, k2) + w_ref[2 * p + h, :, S - n:] + madd
                outs.append(_softmax_pv(s, v2))
            o_ref[0, rows, p * LANES:(p + 1) * LANES] = jnp.where(lo, outs[0], outs[1]).astype(o_ref.dtype)


def _dsa(ua3, ub3, w):
    B, S, _ = ua3.shape
    n_sel = min(DSA_TOPK, S // 4)
    assert n_sel <= BQ
    hw = N_HEADS_DSA * HEAD_DIM
    return pl.pallas_call(
        functools.partial(_dsa_kernel, S=S, n_sel=n_sel),
        grid=(B,),
        in_specs=[
            pl.BlockSpec((1, S, hw), lambda b: (b, 0, SQ_BLK * LANES // hw)),
            pl.BlockSpec((1, S, hw), lambda b: (b, 0, SK_BLK * LANES // hw)),
            pl.BlockSpec((1, S, hw), lambda b: (b, 0, SV_BLK * LANES // hw)),
            pl.BlockSpec((1, S, IDX_HEADS * IDX_DIM), lambda b: (b, 0, IQ_BLK * LANES // (IDX_HEADS * IDX_DIM))),
            pl.BlockSpec((1, S, LANES), lambda b: (b, 0, IK_BLK)),
            pl.BlockSpec((1, S, LANES), lambda b: (b, 0, (MLA_Q_RANK + MLA_KV_RANK) // LANES)),
            pl.BlockSpec((N_HEADS_DSA, BQ, S), lambda b: (0, 0, 0), pipeline_mode=pl.Buffered(1)),
        ],
        out_specs=pl.BlockSpec((1, S, hw), lambda b: (b, 0, 0)),
        out_shape=jax.ShapeDtypeStruct((B, S, hw), BF16),
        scratch_shapes=[pltpu.VMEM((BQ, (i + 1) * BQ), dt) for dt in (jnp.int32, jnp.int16)
                        for i in range(S // BQ)],
        compiler_params=_cparams("parallel"),
        name="dsa_attn",
    )(ua3, ua3, ua3, ua3, ua3, ub3, w)


def _mla_prep_kernel(ub_ref, qn_ref, kvn_ref, wq_ref, wk_ref, wv_ref, c_ref, s1_ref, s2_ref,
                     q_ref, k_ref, v_ref, *, scale):
    ub = ub_ref[...]
    cq = ub[:, :MLA_Q_RANK]
    ckv = ub[:, MLA_Q_RANK:MLA_Q_RANK + MLA_KV_RANK]
    kr_chunk = ub[:, MLA_Q_RANK + MLA_KV_RANK:]

    def rms(t, g):
        return t * lax.rsqrt(jnp.mean(t * t, axis=-1, keepdims=True) + RMS_EPS) * g

    qn = rms(cq, qn_ref[...]).astype(BF16)
    kvn = rms(ckv, kvn_ref[...]).astype(BF16)
    q = jnp.dot(qn, wq_ref[...], preferred_element_type=F32)
    kn = jnp.dot(kvn, wk_ref[...], preferred_element_type=F32)
    v = jnp.dot(kvn, wv_ref[...], preferred_element_type=F32)
    c, s1, s2 = c_ref[...], s1_ref[...], s2_ref[...]

    def rope(t):
        return t * c + pltpu.roll(t, 16, 1) * s1 + pltpu.roll(t, LANES - 16, 1) * s2

    lane = lax.broadcasted_iota(jnp.int32, (1, LANES), 1)
    kr = jnp.where((lane >= MLA_NOPE) & (lane < MLA_NOPE + MLA_ROPE), rope(kr_chunk), 0.0)
    for h in range(N_HEADS_MLA):
        cols = slice(h * LANES, (h + 1) * LANES)
        q_ref[:, cols] = (rope(q[:, cols]) * scale).astype(BF16)
        k_ref[:, cols] = (kn[:, cols] + kr).astype(BF16)
    v_ref[...] = v.astype(BF16)


def _mla_prep(ub, qn, kvn, wq, wk, wv, c, s1, s2, tm=512):
    T = ub.shape[0]
    scale = (MLA_NOPE + MLA_ROPE) ** -0.5
    row = lambda i: (i, 0)
    fix = lambda i: (0, 0)
    hq = N_HEADS_MLA * LANES
    hv = N_HEADS_MLA * MLA_V
    return pl.pallas_call(
        functools.partial(_mla_prep_kernel, scale=scale),
        grid=(T // tm,),
        in_specs=[
            pl.BlockSpec((tm, WB_COLS), row),
            pl.BlockSpec((1, MLA_Q_RANK), fix), pl.BlockSpec((1, MLA_KV_RANK), fix),
            pl.BlockSpec((MLA_Q_RANK, hq), fix), pl.BlockSpec((MLA_KV_RANK, hq), fix),
            pl.BlockSpec((MLA_KV_RANK, hv), fix),
            pl.BlockSpec((tm, LANES), row), pl.BlockSpec((tm, LANES), row), pl.BlockSpec((tm, LANES), row),
        ],
        out_specs=[pl.BlockSpec((tm, hq), row), pl.BlockSpec((tm, hq), row), pl.BlockSpec((tm, hv), row)],
        out_shape=[jax.ShapeDtypeStruct((T, hq), BF16), jax.ShapeDtypeStruct((T, hq), BF16),
                   jax.ShapeDtypeStruct((T, hv), BF16)],
        compiler_params=_cparams("parallel"),
        name="mla_prep",
    )(ub, qn, kvn, wq, wk, wv, c, s1, s2)


def _layer_norm(y, g, b):
    mu = jnp.mean(y, axis=-1, keepdims=True)
    yc = y - mu
    var = jnp.mean(yc * yc, axis=-1, keepdims=True)
    return yc * lax.rsqrt(var + LN_EPS) * g + b


def _outproj_kernel(oa_ref, ob_ref, oc_ref, wa_ref, wb_ref, wc_ref, x_ref, g_ref, b_ref, o_ref, *, alpha):
    acc = jnp.dot(oa_ref[...], wa_ref[...], preferred_element_type=F32)
    acc += jnp.dot(ob_ref[...], wb_ref[...], preferred_element_type=F32)
    acc += jnp.dot(oc_ref[...], wc_ref[...], preferred_element_type=F32)
    o_ref[...] = _layer_norm(alpha * x_ref[...] + acc, g_ref[...], b_ref[...])


def _outproj_ln(oa, ob, oc, wa, wb, wc, x, g, b, alpha, tm=512):
    T, D = x.shape
    row = lambda i: (i, 0)
    fix = lambda i: (0, 0)
    return pl.pallas_call(
        functools.partial(_outproj_kernel, alpha=alpha),
        grid=(T // tm,),
        in_specs=[pl.BlockSpec((tm, oa.shape[1]), row), pl.BlockSpec((tm, ob.shape[1]), row),
                  pl.BlockSpec((tm, oc.shape[1]), row),
                  pl.BlockSpec(wa.shape, fix), pl.BlockSpec(wb.shape, fix), pl.BlockSpec(wc.shape, fix),
                  pl.BlockSpec((tm, D), row), pl.BlockSpec((1, D), fix), pl.BlockSpec((1, D), fix)],
        out_specs=pl.BlockSpec((tm, D), row),
        out_shape=jax.ShapeDtypeStruct((T, D), F32),
        compiler_params=_cparams("parallel"),
        name="out_proj_ln",
    )(oa, ob, oc, wa, wb, wc, x, g, b)


MOE_TT = 256
MOE_CH = 8
MOE_BM = 1024
MOE_SLOTS = -(-(TOP_K * MOE_TT + N_EXPERTS * (MOE_CH - 1)) // MOE_TT) * MOE_TT
MOE_MAXCH = MOE_SLOTS // MOE_CH
MOE_RUNS = (4, 2, 1)
MOE_RUN_CAPS = (MOE_MAXCH // 4, N_EXPERTS, N_EXPERTS)
MOE_NRUN = sum(MOE_RUN_CAPS)
MOE_ZROWS = 256
HALF = D_MODEL // 2
HI_MASK = -65536


def _pack_bf16_pairs(t):
    lo = lax.bitcast_convert_type(t[:, :HALF], jnp.int32)
    hi = lax.bitcast_convert_type(t[:, HALF:], jnp.int32)
    return (hi & jnp.int32(HI_MASK)) | lax.shift_right_logical(lo, jnp.int32(16))


def _unpack_bf16_pairs(w):
    lo = lax.bitcast_convert_type(lax.shift_left(w, jnp.int32(16)), F32)
    hi = lax.bitcast_convert_type(w & jnp.int32(HI_MASK), F32)
    return jnp.concatenate([lo, hi], axis=1).astype(BF16)


def _route_kernel(x_ref, rw_ref, rb_ref, slot_ref, gate_ref, slott_ref, meta_ref):
    tt = x_ref.shape[0]
    logits = jnp.dot(x_ref[...].astype(BF16), rw_ref[...], preferred_element_type=F32)
    scores = 1.0 / (1.0 + jnp.exp(-logits))
    lane = lax.broadcasted_iota(jnp.int32, logits.shape, 1)
    lane_f = lane.astype(F32)
    work = jnp.where(lane < N_EXPERTS, scores + rb_ref[...], NEG)
    sel = lane < 0
    firsts = []
    for _ in range(TOP_K):
        m = jnp.max(work, axis=1, keepdims=True)
        first = jnp.min(jnp.where(work == m, lane_f, float(LANES)), axis=1, keepdims=True)
        pick = lane_f == first
        sel = sel | pick
        work = jnp.where(pick, NEG, work)
        firsts.append(first)
    gs = jnp.where(sel, scores, 0.0)
    gates = gs / jnp.sum(gs, axis=1, keepdims=True) * ROUTED_SCALE

    sel_f = jnp.where(sel, 1.0, 0.0).astype(F32)
    r_i = lax.broadcasted_iota(jnp.int32, (tt, tt), 0)
    c_i = lax.broadcasted_iota(jnp.int32, (tt, tt), 1)
    below = jnp.where(r_i > c_i, 1.0, 0.0).astype(BF16)
    rank = jnp.dot(below, sel_f.astype(BF16), preferred_element_type=F32)
    cnt = jnp.sum(sel_f, axis=0, keepdims=True)
    cnt_pad = jnp.floor((cnt + (MOE_CH - 1)) * (1.0 / MOE_CH)) * MOE_CH
    e_r = lax.broadcasted_iota(jnp.int32, (LANES, LANES), 0)
    e_c = lax.broadcasted_iota(jnp.int32, (LANES, LANES), 1)
    before = jnp.where(e_r < e_c, 1.0, 0.0).astype(BF16)
    base = jnp.dot(jnp.broadcast_to(cnt_pad, (8, LANES)).astype(BF16), before,
                   preferred_element_type=F32)[0:1]
    slot = base + rank

    slot_tk = jnp.full(logits.shape, -1.0, F32)
    gate_tk = jnp.zeros(logits.shape, F32)
    for k, first in enumerate(firsts):
        pick = lane_f == first
        slot_k = jnp.sum(jnp.where(pick, slot, 0.0), axis=1, keepdims=True)
        gate_k = jnp.sum(jnp.where(pick, gates, 0.0), axis=1, keepdims=True)
        slot_tk = jnp.where(lane == k, slot_k, slot_tk)
        gate_tk = jnp.where(lane == k, gate_k, gate_tk)
    slot_ref[...] = slot_tk.astype(jnp.int32)
    gate_ref[...] = gate_tk
    slott_ref[0] = slot_tk.T[:TOP_K].astype(jnp.int32)
    row = lax.broadcasted_iota(jnp.int32, (8, LANES), 0)
    meta = jnp.where(row == 0, jnp.broadcast_to(cnt_pad, (8, LANES)), jnp.broadcast_to(base, (8, LANES)))
    meta_ref[0] = meta.astype(jnp.int32)


def _route(x, rw, rb):
    T, D = x.shape
    nt = T // MOE_TT
    return pl.pallas_call(
        _route_kernel,
        grid=(nt,),
        in_specs=[pl.BlockSpec((MOE_TT, D), lambda i: (i, 0)), pl.BlockSpec((D, LANES), lambda i: (0, 0)),
                  pl.BlockSpec((1, LANES), lambda i: (0, 0))],
        out_specs=[pl.BlockSpec((MOE_TT, LANES), lambda i: (i, 0)), pl.BlockSpec((MOE_TT, LANES), lambda i: (i, 0)),
                   pl.BlockSpec((1, TOP_K, MOE_TT), lambda i: (i, 0, 0)), pl.BlockSpec((1, 8, LANES), lambda i: (i, 0, 0))],
        out_shape=[jax.ShapeDtypeStruct((T, LANES), jnp.int32), jax.ShapeDtypeStruct((T, LANES), F32),
                   jax.ShapeDtypeStruct((nt, TOP_K, MOE_TT), jnp.int32), jax.ShapeDtypeStruct((nt, 8, LANES), jnp.int32)],
        compiler_params=_cparams("parallel"),
        name="moe_route",
    )(x, rw, rb)


def _chunk_copy(src_ref, dst_ref, sem):
    return pltpu.make_async_copy(src_ref, dst_ref, sem)


def _wait_n(n, wait_one):
    def eight(_, c):
        for _ in range(8):
            wait_one()
        return c

    def one(_, c):
        wait_one()
        return c
    lax.fori_loop(0, lax.shift_right_logical(n, 3), eight, 0)
    lax.fori_loop(0, n & 7, one, 0)


def _for_each_run(t, src_ref, dst_ref, num_ref, fn):
    off = 0
    for ci, (run, cap) in enumerate(zip(MOE_RUNS, MOE_RUN_CAPS)):
        def body(p, c, off=off, rows=run * MOE_CH):
            k = t * MOE_NRUN + off + p
            fn(rows, pl.multiple_of(src_ref[k], MOE_CH), pl.multiple_of(dst_ref[k], MOE_CH))
            return c
        lax.fori_loop(0, num_ref[t * len(MOE_RUNS) + ci], body, 0)
        off += cap


def _wait_runs(t, num_ref, wait_one):
    for ci, run in enumerate(MOE_RUNS):
        _wait_n(num_ref[t * len(MOE_RUNS) + ci], functools.partial(wait_one, run * MOE_CH))


def _dispatch_kernel(src_ref, dst_ref, num_ref, tail_start_ref, tail_n_ref,
                     x_ref, slott_ref, xs_ref, loc_ref, zero_ref, sem, tail_sem):
    i = pl.program_id(0)
    nt = pl.num_programs(0)
    cur = i % 2
    s_iota = lax.broadcasted_iota(jnp.int32, (MOE_SLOTS, MOE_TT), 0).astype(jnp.int16)
    slots = slott_ref[0].astype(jnp.int16)
    hit = s_iota == slots[0:1, :]
    for k in range(1, TOP_K):
        hit = hit | (s_iota == slots[k:k + 1, :])
    onehot = jnp.where(hit, jnp.ones((), BF16), jnp.zeros((), BF16))
    rows = jnp.dot(onehot, x_ref[...].astype(BF16), preferred_element_type=F32)
    loc_ref[cur] = _pack_bf16_pairs(rows)

    def run_copy(buf, rows, local, glob):
        return _chunk_copy(loc_ref.at[buf, pl.ds(local, rows), :], xs_ref.at[pl.ds(glob, rows), :], sem)

    @pl.when(i > 0)
    def _():
        _wait_runs(i - 1, num_ref, lambda rows: run_copy(0, rows, 0, 0).wait())

    _for_each_run(i, src_ref, dst_ref, num_ref, lambda rows, local, glob: run_copy(cur, rows, local, glob).start())

    @pl.when(i == nt - 1)
    def _():
        zero_ref[...] = jnp.zeros_like(zero_ref)

        def fill(rows_per_copy, first, last):
            z = zero_ref.at[pl.ds(0, rows_per_copy), :]

            def region(e, c):
                g = tail_start_ref[e]

                def start(j, c2):
                    row = pl.multiple_of(g + j * rows_per_copy, MOE_CH)
                    _chunk_copy(z, xs_ref.at[pl.ds(row, rows_per_copy), :], tail_sem).start()
                    return c2
                return lax.fori_loop(0, tail_n_ref[e], start, c)
            lax.fori_loop(first, last, region, 0)

            def region_wait(e, c):
                _wait_n(tail_n_ref[e], lambda: _chunk_copy(z, xs_ref.at[pl.ds(0, rows_per_copy), :],
                                                           tail_sem).wait())
                return c
            lax.fori_loop(first, last, region_wait, 0)

        fill(MOE_CH, 0, N_EXPERTS)
        fill(MOE_ZROWS, N_EXPERTS, N_EXPERTS + 1)
        _wait_runs(i, num_ref, lambda rows: run_copy(0, rows, 0, 0).wait())


def _dispatch(x, slot_kt, tables, n_rows):
    T, D = x.shape
    nt = T // MOE_TT
    src, dst, num, tail_start, tail_n = tables
    return pl.pallas_call(
        _dispatch_kernel,
        grid_spec=pltpu.PrefetchScalarGridSpec(
            num_scalar_prefetch=5,
            grid=(nt,),
            in_specs=[pl.BlockSpec((MOE_TT, D), lambda i, *_: (i, 0)),
                      pl.BlockSpec((1, TOP_K, MOE_TT), lambda i, *_: (i, 0, 0))],
            out_specs=pl.BlockSpec(memory_space=pl.ANY),
            scratch_shapes=[pltpu.VMEM((2, MOE_SLOTS, HALF), jnp.int32), pltpu.VMEM((MOE_ZROWS, HALF), jnp.int32),
                            pltpu.SemaphoreType.DMA(()), pltpu.SemaphoreType.DMA(())],
        ),
        out_shape=jax.ShapeDtypeStruct((n_rows, HALF), jnp.int32),
        compiler_params=_cparams("arbitrary"),
        name="moe_dispatch",
    )(src, dst, num, tail_start, tail_n, x, slot_kt)


def _expert_kernel(be_ref, nu_ref, xs_ref, w1_ref, w3_ref, w2_ref, ys_ref, w1b_ref, w3b_ref, w2b_ref):
    i = pl.program_id(0)

    @pl.when((i == 0) | (be_ref[i] != be_ref[jnp.maximum(i - 1, 0)]))
    def _():
        w1b_ref[...] = w1_ref[0].astype(BF16)
        w3b_ref[...] = w3_ref[0].astype(BF16)
        w2b_ref[...] = w2_ref[0].astype(BF16)

    @pl.when(i < nu_ref[0])
    def _():
        xb = _unpack_bf16_pairs(xs_ref[...])
        h1 = jnp.dot(xb, w1b_ref[...], preferred_element_type=F32)
        h3 = jnp.dot(xb, w3b_ref[...], preferred_element_type=F32)
        h = h1 / (1.0 + jnp.exp(-h1)) * h3
        y = jnp.dot(h.astype(BF16), w2b_ref[...], preferred_element_type=F32)
        ys_ref[...] = _pack_bf16_pairs(y.astype(BF16).astype(F32))

    @pl.when(i >= nu_ref[0])
    def _():
        ys_ref[...] = jnp.zeros_like(ys_ref)


def _experts(xs, block_expert, n_used, w1, w3, w2):
    n_rows = xs.shape[0]
    nb = n_rows // MOE_BM
    _, D, ff = w1.shape
    blk = lambda i, be, nu: (jnp.minimum(i, nu[0] - 1), 0)
    return pl.pallas_call(
        _expert_kernel,
        grid_spec=pltpu.PrefetchScalarGridSpec(
            num_scalar_prefetch=2,
            grid=(nb,),
            in_specs=[pl.BlockSpec((MOE_BM, HALF), blk),
                      pl.BlockSpec((1, D, ff), lambda i, be, nu: (be[i], 0, 0)),
                      pl.BlockSpec((1, D, ff), lambda i, be, nu: (be[i], 0, 0)),
                      pl.BlockSpec((1, ff, D), lambda i, be, nu: (be[i], 0, 0))],
            out_specs=pl.BlockSpec((MOE_BM, HALF), lambda i, be, nu: (i, 0)),
            scratch_shapes=[pltpu.VMEM((D, ff), BF16), pltpu.VMEM((D, ff), BF16), pltpu.VMEM((ff, D), BF16)],
        ),
        out_shape=jax.ShapeDtypeStruct((n_rows, HALF), jnp.int32),
        compiler_params=_cparams("arbitrary"),
        name="moe_experts",
    )(block_expert, n_used, xs, w1, w3, w2)


def _combine_kernel(src_ref, dst_ref, num_ref, slot_ref, gate_ref, ys_ref, o_ref, loc_ref, sem):
    i = pl.program_id(0)
    nt = pl.num_programs(0)
    cur = i % 2

    def run_copy(buf, rows, local, glob):
        return _chunk_copy(ys_ref.at[pl.ds(glob, rows), :], loc_ref.at[buf, pl.ds(local, rows), :], sem.at[buf])

    def fetch(t, buf):
        _for_each_run(t, src_ref, dst_ref, num_ref, lambda rows, local, glob: run_copy(buf, rows, local, glob).start())

    @pl.when(i == 0)
    def _():
        loc_ref[...] = jnp.zeros_like(loc_ref)
        fetch(0, 0)

    @pl.when(i + 1 < nt)
    def _():
        fetch(i + 1, 1 - cur)

    _wait_runs(i, num_ref, lambda rows: run_copy(cur, rows, 0, 0).wait())

    s_iota = lax.broadcasted_iota(jnp.int32, (MOE_TT, MOE_SLOTS), 1)
    slots = slot_ref[...]
    gates = gate_ref[...]
    wgt = jnp.zeros((MOE_TT, MOE_SLOTS), F32)
    for k in range(TOP_K):
        wgt = jnp.where(s_iota == slots[:, k:k + 1], gates[:, k:k + 1], wgt)
    w_hi = wgt.astype(BF16)
    w_lo = (wgt - w_hi.astype(F32)).astype(BF16)
    ys = _unpack_bf16_pairs(loc_ref[cur])
    o_ref[...] = (jnp.dot(w_hi, ys, preferred_element_type=F32) + jnp.dot(w_lo, ys, preferred_element_type=F32))


def _combine(ys, slot_tk, gate_tk, tables):
    T = slot_tk.shape[0]
    nt = T // MOE_TT
    src, dst, num = tables
    return pl.pallas_call(
        _combine_kernel,
        grid_spec=pltpu.PrefetchScalarGridSpec(
            num_scalar_prefetch=3,
            grid=(nt,),
            in_specs=[pl.BlockSpec((MOE_TT, LANES), lambda i, *_: (i, 0)),
                      pl.BlockSpec((MOE_TT, LANES), lambda i, *_: (i, 0)),
                      pl.BlockSpec(memory_space=pl.ANY)],
            out_specs=pl.BlockSpec((MOE_TT, D_MODEL), lambda i, *_: (i, 0)),
            scratch_shapes=[pltpu.VMEM((2, MOE_SLOTS, HALF), jnp.int32), pltpu.SemaphoreType.DMA((2,))],
        ),
        out_shape=jax.ShapeDtypeStruct((T, D_MODEL), F32),
        compiler_params=_cparams("arbitrary"),
        name="moe_combine",
    )(src, dst, num, slot_tk, gate_tk, ys)


def _shared_ln_kernel(x_ref, r_ref, w1_ref, w3_ref, w2_ref, g_ref, b_ref, o_ref, *, alpha):
    x = x_ref[...]
    xb = x.astype(BF16)
    h1 = jnp.dot(xb, w1_ref[...], preferred_element_type=F32)
    h3 = jnp.dot(xb, w3_ref[...], preferred_element_type=F32)
    h = h1 / (1.0 + jnp.exp(-h1)) * h3
    y = jnp.dot(h.astype(BF16), w2_ref[...], preferred_element_type=F32)
    o_ref[...] = _layer_norm(alpha * x + (r_ref[...] + y), g_ref[...], b_ref[...])


def _shared_ln(x, routed, w1, w3, w2, g, b, alpha, tm=512):
    T, D = x.shape
    row = lambda i: (i, 0)
    fix = lambda i: (0, 0)
    return pl.pallas_call(
        functools.partial(_shared_ln_kernel, alpha=alpha),
        grid=(T // tm,),
        in_specs=[pl.BlockSpec((tm, D), row), pl.BlockSpec((tm, D), row), pl.BlockSpec(w1.shape, fix),
                  pl.BlockSpec(w3.shape, fix), pl.BlockSpec(w2.shape, fix), pl.BlockSpec((1, D), fix),
                  pl.BlockSpec((1, D), fix)],
        out_specs=pl.BlockSpec((tm, D), row),
        out_shape=jax.ShapeDtypeStruct((T, D), F32),
        compiler_params=_cparams("parallel"),
        name="shared_ffn_ln",
    )(x, routed, w1, w3, w2, g, b)


def _dispatch_tables(meta, n_blocks):
    cnt = meta[:, 0, :N_EXPERTS]
    base = meta[:, 1, :N_EXPERTS]
    tot = jnp.sum(cnt, axis=0)
    nblk = (tot + MOE_BM - 1) // MOE_BM
    bend = jnp.cumsum(nblk)
    pstart = (bend - nblk) * MOE_BM
    gstart = pstart[None, :] + jnp.cumsum(cnt, axis=0) - cnt
    n_used = bend[-1:]
    blocks = jnp.arange(n_blocks, dtype=jnp.int32)
    be = jnp.sum((jnp.minimum(blocks, n_used - 1)[:, None] >= bend[None, :]).astype(jnp.int32), axis=1)
    chunks = cnt // MOE_CH
    done = jnp.zeros_like(chunks)
    src_l, dst_l, num_l = [], [], []
    for run, cap in zip(MOE_RUNS, MOE_RUN_CAPS):
        m = (chunks - done) // run
        cum = jnp.cumsum(m, axis=1) - m
        p = jnp.arange(cap, dtype=jnp.int32)
        owned = cum[:, None, :] <= p[None, :, None]
        last = owned & ~jnp.concatenate([owned[:, :, 1:], jnp.zeros_like(owned[:, :, :1])], axis=-1)
        pick = lambda t: jnp.sum(jnp.where(last, t[:, None, :], 0), axis=-1)
        within = (p[None, :] - pick(cum)) * (run * MOE_CH) + pick(done) * MOE_CH
        src_l.append(pick(base) + within)
        dst_l.append(pick(gstart) + within)
        num_l.append(jnp.sum(m, axis=1))
        done = done + m * run
    i32 = lambda a: a.astype(jnp.int32)
    common = (i32(jnp.concatenate(src_l, axis=1).reshape(-1)), i32(jnp.concatenate(dst_l, axis=1).reshape(-1)),
              i32(jnp.stack(num_l, axis=1).reshape(-1)))
    tails = (i32(jnp.concatenate([pstart + tot, n_used * MOE_BM])),
             i32(jnp.concatenate([(nblk * MOE_BM - tot) // MOE_CH, (n_blocks - n_used) * (MOE_BM // MOE_ZROWS)])))
    return common, tails, i32(jnp.minimum(be, N_EXPERTS - 1)), i32(n_used)


def _moe_block(x1, rw, rb, w1, w3, w2, sw1, sw3, sw2, lg, lb, alpha):
    T = x1.shape[0]
    nt = T // MOE_TT
    n_blocks = -(-(TOP_K * T + nt * N_EXPERTS * (MOE_CH - 1)) // MOE_BM) + N_EXPERTS
    slot_tk, gate_tk, slot_kt, meta = _route(x1, rw, rb)
    common, tails, be, n_used = _dispatch_tables(meta, n_blocks)
    xs = _dispatch(x1, slot_kt, common + tails, n_blocks * MOE_BM)
    ys = _experts(xs, be, n_used, w1, w3, w2)
    routed = _combine(ys, slot_tk, gate_tk, common)
    return _shared_ln(x1, routed, sw1, sw3, sw2, lg, lb, alpha)


def _t5_bucket(dist):
    max_exact = N_BUCKETS // 2
    d_f = jnp.maximum(dist, 1).astype(F32)
    large = max_exact + (jnp.log(d_f / max_exact) / math.log(MAX_DISTANCE / max_exact)
                         * (N_BUCKETS - max_exact)).astype(jnp.int32)
    large = jnp.minimum(large, N_BUCKETS - 1)
    return jnp.where(dist < max_exact, dist, large)


def _bias_tables(rel_bias, S):
    dist = np.arange(S)
    e = rel_bias[_t5_bucket(jnp.asarray(dist, jnp.int32))].T
    mult = sum(((dist % d == 0) & (dist <= w)).astype(np.float32) for w, d in DILATIONS)
    e_dil = jnp.where(mult > 0, e[:N_HEADS_DIL] + jnp.log(jnp.maximum(mult, 1.0)), NEG)
    e_all = jnp.concatenate([e_dil, e[N_HEADS_DIL:], jnp.zeros((1, S), F32)], 0)
    n_h = e_all.shape[0]
    a = jnp.concatenate([e_all[:, :S - BQ + 1][:, ::-1], jnp.full((n_h, BQ), NEG, F32),
                         e_all[:, S - BQ + 1:][:, ::-1]], 1)
    w = _toeplitz(a.astype(F32), S)
    return w[:N_HEADS_DIL], w[N_HEADS_DIL:N_HEADS_DIL + N_HEADS_DSA], w[-1:]


def _toeplitz_kernel(a_ref, o_ref):
    period = a_ref.shape[-1]
    rolled = pltpu.roll(jnp.broadcast_to(a_ref[0], (BQ, period)), 0, 1, stride=1, stride_axis=0)
    o_ref[0] = rolled[:, :o_ref.shape[-1]]


def _toeplitz(a, S):
    n_h, period = a.shape
    return pl.pallas_call(
        _toeplitz_kernel,
        grid=(n_h,),
        in_specs=[pl.BlockSpec((1, 1, period), lambda h: (h, 0, 0))],
        out_specs=pl.BlockSpec((1, BQ, S), lambda h: (h, 0, 0)),
        out_shape=jax.ShapeDtypeStruct((n_h, BQ, S), F32),
        compiler_params=_cparams("parallel"),
        name="bias_toeplitz",
    )(a.reshape(n_h, 1, period))


def _rope_tables(positions):
    inv = ROPE_THETA ** (-jnp.arange(0, MLA_ROPE, 2, dtype=F32) / MLA_ROPE)
    ang = positions.astype(F32).reshape(-1, 1) * inv
    cos, sin = jnp.cos(ang), jnp.sin(ang)
    T, half = cos.shape
    one = jnp.ones((T, MLA_NOPE), F32)
    z = lambda n: jnp.zeros((T, n), F32)
    tail = LANES - MLA_NOPE - MLA_ROPE
    c = jnp.concatenate([one, cos, cos, jnp.ones((T, tail), F32)], 1)
    s1 = jnp.concatenate([z(MLA_NOPE + half), sin, z(tail)], 1)
    s2 = jnp.concatenate([z(MLA_NOPE), -sin, z(half + tail)], 1)
    return c, s1, s2


def _split_w_in(w):
    o = np.cumsum([0, 384, 384, 384, 256, 256, 256, 512, 64, 8, 256, 128, 32])
    ik = w[:, o[7]:o[8]]
    wa = jnp.concatenate([w[:, o[6]:o[7]], w[:, o[3]:o[6]], ik, ik, w[:, :o[3]]], 1)
    z = lambda n: jnp.zeros((w.shape[0], n), w.dtype)
    chunk = jnp.concatenate([w[:, o[8]:o[9]], z(MLA_NOPE - IDX_HEADS), w[:, o[11]:o[12]],
                             z(LANES - MLA_NOPE - MLA_ROPE)], 1)
    wb = jnp.concatenate([w[:, o[9]:o[11]], chunk], 1)
    return wa.astype(BF16), wb.astype(BF16)


def _split_mla_weights(w_uq, w_ukv):
    r = w_uq.shape[0]
    wq = w_uq.reshape(r, N_HEADS_MLA, MLA_NOPE + MLA_ROPE)
    wq = jnp.pad(wq, ((0, 0), (0, 0), (0, LANES - MLA_NOPE - MLA_ROPE))).reshape(r, N_HEADS_MLA * LANES)
    r = w_ukv.shape[0]
    wkv = w_ukv.reshape(r, N_HEADS_MLA, MLA_NOPE + MLA_V)
    wk = jnp.pad(wkv[:, :, :MLA_NOPE], ((0, 0), (0, 0), (0, LANES - MLA_NOPE))).reshape(r, N_HEADS_MLA * LANES)
    wv = wkv[:, :, MLA_NOPE:].reshape(r, N_HEADS_MLA * MLA_V)
    return wq.astype(BF16), wk.astype(BF16), wv.astype(BF16)


def kernel(x, positions, w_in, mla_q_norm, mla_w_uq, mla_kv_norm, mla_w_ukv, w_out, ln1_g, ln1_b, router_w,
           router_bias, exp_w1, exp_w3, exp_w2, sh_w1, sh_w3, sh_w2, ln2_g, ln2_b, rel_bias):
    B, S, D = x.shape
    T = B * S
    depth = w_in.shape[0]
    alpha = (2 * depth) ** 0.25
    w_dil, w_dsa, w_causal = _bias_tables(rel_bias, S)
    rope_c, rope_s1, rope_s2 = _rope_tables(positions)
    na, nb = N_HEADS_DIL * HEAD_DIM, (N_HEADS_DIL + N_HEADS_DSA) * HEAD_DIM
    xf = x.reshape(T, D)
    for l in range(depth):
        wa, wb = _split_w_in(w_in[l])
        ua = _matmul(xf, wa, BF16)
        ub = _matmul(xf, wb, F32)
        ua3 = ua.reshape(B, S, WA_COLS)
        o_a = _attention(ua3, ua3, ua3, w_dil, q_blk=DQ_BLK, k_blk=DK_BLK, v_blk=DV_BLK,
                         n_pairs=N_HEADS_DIL // 2, packed=True, per_head_w=True, name="dilated_attn")
        o_b = _dsa(ua3, ub.reshape(B, S, WB_COLS), w_dsa)
        wq, wk, wv = _split_mla_weights(mla_w_uq[l], mla_w_ukv[l])
        q, k, v = _mla_prep(ub, mla_q_norm[l][None], mla_kv_norm[l][None], wq, wk, wv, rope_c, rope_s1, rope_s2)
        o_c = _attention(q.reshape(B, S, -1), k.reshape(B, S, -1), v.reshape(B, S, -1), w_causal,
                         q_blk=0, k_blk=0, v_blk=0, n_pairs=N_HEADS_MLA // 2, packed=False, per_head_w=False,
                         name="mla_attn")
        wo = w_out[l].astype(BF16)
        x1 = _outproj_ln(o_a.reshape(T, -1), o_b.reshape(T, -1), o_c.reshape(T, -1),
                         wo[:na], wo[na:nb], wo[nb:], xf, ln1_g[l][None], ln1_b[l][None], alpha)
        rw = jnp.pad(router_w[l], ((0, 0), (0, LANES - N_EXPERTS))).astype(BF16)
        rb = jnp.pad(router_bias[l].astype(F32), (0, LANES - N_EXPERTS))[None]
        xf = _moe_block(x1, rw, rb, exp_w1[l], exp_w3[l], exp_w2[l], sh_w1[l].astype(BF16), sh_w3[l].astype(BF16),
                        sh_w2[l].astype(BF16), ln2_g[l][None], ln2_b[l][None], alpha)
    return xf.reshape(B, S, D)
```

```python
import functools
import math

import jax
import jax.numpy as jnp
import numpy as np
from jax import lax
from jax.experimental import pallas as pl
from jax.experimental.pallas import tpu as pltpu

F32 = jnp.float32
BF16 = jnp.bfloat16

D_MODEL = 1024
HEAD_DIM = 64
N_HEADS_DIL = 6
N_HEADS_DSA = 4
N_HEADS_MLA = 6
DILATIONS = ((128, 1), (512, 4), (2048, 16))
IDX_HEADS = 8
IDX_DIM = 64
DSA_TOPK = 256
MLA_Q_RANK = 256
MLA_KV_RANK = 128
MLA_NOPE = 64
MLA_ROPE = 32
MLA_V = 64
ROPE_THETA = 10000.0
N_BUCKETS = 32
MAX_DISTANCE = 2048
N_EXPERTS = 64
TOP_K = 8
EXPERT_FF = 256
ROUTED_SCALE = 2.5
LN_EPS = 1e-5
RMS_EPS = 1e-6

LANES = 128
NEG = -1e30
INT_MIN = -2147483648
VMEM_LIMIT = 56 * 1024 * 1024
BQ = 256

WA_COLS = 3 * N_HEADS_DIL * HEAD_DIM + 3 * N_HEADS_DSA * HEAD_DIM + IDX_HEADS * IDX_DIM + 2 * IDX_DIM
WB_COLS = MLA_Q_RANK + MLA_KV_RANK + LANES
IQ_BLK = 0
SQ_BLK, SK_BLK, SV_BLK = 4, 6, 8
IK_BLK = 10
DQ_BLK, DK_BLK, DV_BLK = 11, 14, 17


def _cparams(*sem):
    return pltpu.CompilerParams(dimension_semantics=sem, vmem_limit_bytes=VMEM_LIMIT)


def _mm_kernel(x_ref, w_ref, o_ref):
    o_ref[...] = jnp.dot(x_ref[...].astype(BF16), w_ref[...], preferred_element_type=F32).astype(o_ref.dtype)


def _matmul(x, w, out_dtype, tm=512):
    T, K = x.shape
    N = w.shape[1]
    return pl.pallas_call(
        _mm_kernel,
        grid=(T // tm,),
        in_specs=[pl.BlockSpec((tm, K), lambda i: (i, 0)), pl.BlockSpec((K, N), lambda i: (0, 0))],
        out_specs=pl.BlockSpec((tm, N), lambda i: (i, 0)),
        out_shape=jax.ShapeDtypeStruct((T, N), out_dtype),
        compiler_params=_cparams("parallel"),
        name="in_proj",
    )(x, w)


def _head_masks():
    lane = lax.broadcasted_iota(jnp.int32, (1, LANES), 1)
    return lane < HEAD_DIM, lane >= HEAD_DIM


def _select_head(q2, head_lanes):
    return jnp.where(head_lanes, q2 * jnp.asarray(HEAD_DIM ** -0.5, BF16), jnp.zeros((), BF16))


def _softmax_pv(s, v):
    m = jnp.max(s, axis=1, keepdims=True)
    p = jnp.exp(s - m)
    l = jnp.sum(p, axis=1, keepdims=True)
    return jnp.dot(p.astype(BF16), v, preferred_element_type=F32) / l


def _nt_dot(a, b):
    return lax.dot_general(a, b, (((1,), (1,)), ((), ())), preferred_element_type=F32)


def _attn_kernel(q_ref, k_ref, v_ref, w_ref, o_ref, *, S, packed, per_head_w):
    lo, hi = _head_masks()
    for i in range(S // BQ):
        n = (i + 1) * BQ
        rows = slice(i * BQ, (i + 1) * BQ)
        v2 = v_ref[0, :n, :]
        outs = []
        for h in range(2):
            if packed:
                qh = _select_head(q_ref[0, rows, :], lo if h == 0 else hi)
                kh = k_ref[0, :n, :]
            else:
                qh = q_ref[0, rows, h * LANES:(h + 1) * LANES]
                kh = k_ref[0, :n, h * LANES:(h + 1) * LANES]
            s = _nt_dot(qh, kh) + w_ref[h if per_head_w else 0, :, S - n:]
            outs.append(_softmax_pv(s, v2))
        o_ref[0, rows, :] = jnp.where(lo, outs[0], outs[1]).astype(o_ref.dtype)


def _attention(q_arr, k_arr, v_arr, w, *, q_blk, k_blk, v_blk, n_pairs, packed, per_head_w, name):
    B, S, _ = q_arr.shape
    qk_w = LANES if packed else 2 * LANES
    wh = 2 if per_head_w else 1
    return pl.pallas_call(
        functools.partial(_attn_kernel, S=S, packed=packed, per_head_w=per_head_w),
        grid=(n_pairs, B),
        in_specs=[
            pl.BlockSpec((1, S, qk_w), lambda p, b: (b, 0, q_blk + p)),
            pl.BlockSpec((1, S, qk_w), lambda p, b: (b, 0, k_blk + p)),
            pl.BlockSpec((1, S, LANES), lambda p, b: (b, 0, v_blk + p)),
            pl.BlockSpec((wh, BQ, S), (lambda p, b: (p, 0, 0)) if per_head_w else (lambda p, b: (0, 0, 0))),
        ],
        out_specs=pl.BlockSpec((1, S, LANES), lambda p, b: (b, 0, p)),
        out_shape=jax.ShapeDtypeStruct((B, S, n_pairs * LANES), BF16),
        compiler_params=_cparams("parallel", "parallel"),
        name=name,
    )(q_arr, k_arr, v_arr, w)


I16_MIN = -32768
IDX_RC, IDX_CC = 64, 256


def _count16(mask):
    ones = jnp.where(mask, jnp.asarray(1, BF16), jnp.asarray(0, BF16))
    acc = ones[:, :LANES]
    for j in range(1, ones.shape[1] // LANES):
        acc = acc + ones[:, j * LANES:(j + 1) * LANES]
    return jnp.sum(acc.astype(F32), axis=1, keepdims=True)


def _count(mask):
    return jnp.sum(jnp.where(mask, 1.0, 0.0).astype(F32), axis=1, keepdims=True)


def _dsa_kernel(sq_ref, sk_ref, sv_ref, iq_ref, ik_ref, iw_ref, w_ref, o_ref, *scratch, S, n_sel):
    lo, hi = _head_masks()
    idx_bits = (S - 1).bit_length()
    ksel = float(n_sel)
    nq = S // BQ
    key_refs, half_refs = scratch[:nq], scratch[nq:]
    for i in range(nq):
        n = (i + 1) * BQ
        rows = slice(i * BQ, (i + 1) * BQ)
        for r0 in range(0, BQ, IDX_RC):
            qrows = slice(i * BQ + r0, i * BQ + r0 + IDX_RC)
            iw_r = iw_ref[0, qrows, :]
            stacked = jnp.concatenate(
                [jnp.where(lo if hh % 2 == 0 else hi, iq_ref[0, qrows, (hh // 2) * LANES:(hh // 2 + 1) * LANES],
                           jnp.zeros((), BF16)) for hh in range(IDX_HEADS)], axis=0)
            for c0 in range(0, n, IDX_CC):
                rel = _nt_dot(stacked, ik_ref[0, c0:c0 + IDX_CC, :])
                score = jnp.zeros((IDX_RC, IDX_CC), F32)
                for hh in range(IDX_HEADS):
                    score = score + iw_r[:, hh:hh + 1] * jnp.maximum(rel[hh * IDX_RC:(hh + 1) * IDX_RC], 0.0)
                col = lax.broadcasted_iota(jnp.int32, (IDX_RC, IDX_CC), 1) + c0
                row = lax.broadcasted_iota(jnp.int32, (IDX_RC, IDX_CC), 0) + (i * BQ + r0)
                bits = lax.bitcast_convert_type(score, jnp.int32)
                key = bits ^ ((bits >> 31) & jnp.int32(0x7FFFFFFF))
                key = jnp.where(col <= row, key, jnp.int32(INT_MIN))
                key_refs[i][r0:r0 + IDX_RC, c0:c0 + IDX_CC] = key
                half_refs[i][r0:r0 + IDX_RC, c0:c0 + IDX_CC] = (key >> 16).astype(jnp.int16)

    def bisect16(base_counts):
        def ok(i, cand):
            return base_counts[i] + _count16(half_refs[i][...] >= cand.astype(jnp.int16)) >= ksel

        zero = jnp.zeros((BQ, 1), jnp.int32)
        t0 = tuple(jnp.where(ok(i, zero), jnp.int32(0), jnp.int32(I16_MIN)) for i in range(nq))

        def step(it, ts):
            bit = jnp.left_shift(jnp.int32(1), 14 - it)
            return tuple(jnp.where(ok(i, ts[i] | bit), ts[i] | bit, ts[i]) for i in range(nq))

        return lax.fori_loop(0, 15, step, t0)

    tau_hi = bisect16(tuple(jnp.zeros((BQ, 1), F32) for _ in range(nq)))
    above = []
    for i in range(nq):
        hi16 = half_refs[i][...]
        th = tau_hi[i].astype(jnp.int16)
        above.append(_count16(hi16 > th))
        lo16 = ((key_refs[i][...] & jnp.int32(0xFFFF)) - jnp.int32(32768)).astype(jnp.int16)
        half_refs[i][...] = jnp.where(hi16 == th, lo16, jnp.asarray(I16_MIN, jnp.int16))
    tau_lo = bisect16(tuple(above))

    for i in range(nq):
        n = (i + 1) * BQ
        rows = slice(i * BQ, (i + 1) * BQ)
        key = key_refs[i][...]
        tau = tau_hi[i] * jnp.int32(65536) + (tau_lo[i] + jnp.int32(32768))
        col = lax.broadcasted_iota(jnp.int32, (BQ, n), 1)
        gt = key > tau
        eq = key == tau
        need = ksel - _count(gt)
        excess = (_count(eq) > need) & (tau > jnp.int32(INT_MIN))
        any_excess = jnp.max(jnp.where(excess, 1, 0).astype(jnp.int32)) > 0

        def idx_step(it, j, eq=eq, col=col, need=need):
            cand = j + jnp.left_shift(jnp.int32(1), idx_bits - 1 - it)
            return jnp.where(_count(eq & (col < cand)) < need, cand, j)

        jmax = lax.fori_loop(0, jnp.where(any_excess, idx_bits, 0), idx_step, jnp.zeros((BQ, 1), jnp.int32))
        jmax = jnp.where(excess, jmax, jnp.int32(S))
        madd = jnp.where(gt | (eq & (col <= jmax)), 0.0, NEG).astype(F32)

        for p in range(N_HEADS_DSA // 2):
            q2 = sq_ref[0, rows, p * LANES:(p + 1) * LANES]
            k2 = sk_ref[0, :n, p * LANES:(p + 1) * LANES]
            v2 = sv_ref[0, :n, p * LANES:(p + 1) * LANES]
            outs = []
            for h in range(2):
                qh = _select_head(q2, lo if h == 0 else hi)
                s = _nt_dot(qh, k2) + w_ref[2 * p + h, :, S - n:] + madd
                outs.append(_softmax_pv(s, v2))
            o_ref[0, rows, p * LANES:(p + 1) * LANES] = jnp.where(lo, outs[0], outs[1]).astype(o_ref.dtype)


def _dsa(ua3, ub3, w):
    B, S, _ = ua3.shape
    n_sel = min(DSA_TOPK, S // 4)
    assert n_sel <= BQ
    hw = N_HEADS_DSA * HEAD_DIM
    return pl.pallas_call(
        functools.partial(_dsa_kernel, S=S, n_sel=n_sel),
        grid=(B,),
        in_specs=[
            pl.BlockSpec((1, S, hw), lambda b: (b, 0, SQ_BLK * LANES // hw)),
            pl.BlockSpec((1, S, hw), lambda b: (b, 0, SK_BLK * LANES // hw)),
            pl.BlockSpec((1, S, hw), lambda b: (b, 0, SV_BLK * LANES // hw)),
            pl.BlockSpec((1, S, IDX_HEADS * IDX_DIM), lambda b: (b, 0, IQ_BLK * LANES // (IDX_HEADS * IDX_DIM))),
            pl.BlockSpec((1, S, LANES), lambda b: (b, 0, IK_BLK)),
            pl.BlockSpec((1, S, LANES), lambda b: (b, 0, (MLA_Q_RANK + MLA_KV_RANK) // LANES)),
            pl.BlockSpec((N_HEADS_DSA, BQ, S), lambda b: (0, 0, 0), pipeline_mode=pl.Buffered(1)),
        ],
        out_specs=pl.BlockSpec((1, S, hw), lambda b: (b, 0, 0)),
        out_shape=jax.ShapeDtypeStruct((B, S, hw), BF16),
        scratch_shapes=[pltpu.VMEM((BQ, (i + 1) * BQ), dt) for dt in (jnp.int32, jnp.int16)
                        for i in range(S // BQ)],
        compiler_params=_cparams("parallel"),
        name="dsa_attn",
    )(ua3, ua3, ua3, ua3, ua3, ub3, w)


def _mla_prep_kernel(ub_ref, qn_ref, kvn_ref, wq_ref, wk_ref, wv_ref, c_ref, s1_ref, s2_ref,
                     q_ref, k_ref, v_ref, *, scale):
    ub = ub_ref[...]
    cq = ub[:, :MLA_Q_RANK]
    ckv = ub[:, MLA_Q_RANK:MLA_Q_RANK + MLA_KV_RANK]
    kr_chunk = ub[:, MLA_Q_RANK + MLA_KV_RANK:]

    def rms(t, g):
        return t * lax.rsqrt(jnp.mean(t * t, axis=-1, keepdims=True) + RMS_EPS) * g

    qn = rms(cq, qn_ref[...]).astype(BF16)
    kvn = rms(ckv, kvn_ref[...]).astype(BF16)
    q = jnp.dot(qn, wq_ref[...], preferred_element_type=F32)
    kn = jnp.dot(kvn, wk_ref[...], preferred_element_type=F32)
    v = jnp.dot(kvn, wv_ref[...], preferred_element_type=F32)
    c, s1, s2 = c_ref[...], s1_ref[...], s2_ref[...]

    def rope(t):
        return t * c + pltpu.roll(t, 16, 1) * s1 + pltpu.roll(t, LANES - 16, 1) * s2

    lane = lax.broadcasted_iota(jnp.int32, (1, LANES), 1)
    kr = jnp.where((lane >= MLA_NOPE) & (lane < MLA_NOPE + MLA_ROPE), rope(kr_chunk), 0.0)
    for h in range(N_HEADS_MLA):
        cols = slice(h * LANES, (h + 1) * LANES)
        q_ref[:, cols] = (rope(q[:, cols]) * scale).astype(BF16)
        k_ref[:, cols] = (kn[:, cols] + kr).astype(BF16)
    v_ref[...] = v.astype(BF16)


def _mla_prep(ub, qn, kvn, wq, wk, wv, c, s1, s2, tm=512):
    T = ub.shape[0]
    scale = (MLA_NOPE + MLA_ROPE) ** -0.5
    row = lambda i: (i, 0)
    fix = lambda i: (0, 0)
    hq = N_HEADS_MLA * LANES
    hv = N_HEADS_MLA * MLA_V
    return pl.pallas_call(
        functools.partial(_mla_prep_kernel, scale=scale),
        grid=(T // tm,),
        in_specs=[
            pl.BlockSpec((tm, WB_COLS), row),
            pl.BlockSpec((1, MLA_Q_RANK), fix), pl.BlockSpec((1, MLA_KV_RANK), fix),
            pl.BlockSpec((MLA_Q_RANK, hq), fix), pl.BlockSpec((MLA_KV_RANK, hq), fix),
            pl.BlockSpec((MLA_KV_RANK, hv), fix),
            pl.BlockSpec((tm, LANES), row), pl.BlockSpec((tm, LANES), row), pl.BlockSpec((tm, LANES), row),
        ],
        out_specs=[pl.BlockSpec((tm, hq), row), pl.BlockSpec((tm, hq), row), pl.BlockSpec((tm, hv), row)],
        out_shape=[jax.ShapeDtypeStruct((T, hq), BF16), jax.ShapeDtypeStruct((T, hq), BF16),
                   jax.ShapeDtypeStruct((T, hv), BF16)],
        compiler_params=_cparams("parallel"),
        name="mla_prep",
    )(ub, qn, kvn, wq, wk, wv, c, s1, s2)


def _layer_norm(y, g, b):
    mu = jnp.mean(y, axis=-1, keepdims=True)
    yc = y - mu
    var = jnp.mean(yc * yc, axis=-1, keepdims=True)
    return yc * lax.rsqrt(var + LN_EPS) * g + b


def _outproj_kernel(oa_ref, ob_ref, oc_ref, wa_ref, wb_ref, wc_ref, x_ref, g_ref, b_ref, o_ref, *, alpha):
    acc = jnp.dot(oa_ref[...], wa_ref[...], preferred_element_type=F32)
    acc += jnp.dot(ob_ref[...], wb_ref[...], preferred_element_type=F32)
    acc += jnp.dot(oc_ref[...], wc_ref[...], preferred_element_type=F32)
    o_ref[...] = _layer_norm(alpha * x_ref[...] + acc, g_ref[...], b_ref[...])


def _outproj_ln(oa, ob, oc, wa, wb, wc, x, g, b, alpha, tm=512):
    T, D = x.shape
    row = lambda i: (i, 0)
    fix = lambda i: (0, 0)
    return pl.pallas_call(
        functools.partial(_outproj_kernel, alpha=alpha),
        grid=(T // tm,),
        in_specs=[pl.BlockSpec((tm, oa.shape[1]), row), pl.BlockSpec((tm, ob.shape[1]), row),
                  pl.BlockSpec((tm, oc.shape[1]), row),
                  pl.BlockSpec(wa.shape, fix), pl.BlockSpec(wb.shape, fix), pl.BlockSpec(wc.shape, fix),
                  pl.BlockSpec((tm, D), row), pl.BlockSpec((1, D), fix), pl.BlockSpec((1, D), fix)],
        out_specs=pl.BlockSpec((tm, D), row),
        out_shape=jax.ShapeDtypeStruct((T, D), F32),
        compiler_params=_cparams("parallel"),
        name="out_proj_ln",
    )(oa, ob, oc, wa, wb, wc, x, g, b)


MOE_TT = 256
MOE_CH = 8
MOE_BM = 1024
MOE_SLOTS = -(-(TOP_K * MOE_TT + N_EXPERTS * (MOE_CH - 1)) // MOE_TT) * MOE_TT
MOE_MAXCH = MOE_SLOTS // MOE_CH
MOE_RUNS = (4, 2, 1)
MOE_RUN_CAPS = (MOE_MAXCH // 4, N_EXPERTS, N_EXPERTS)
MOE_NRUN = sum(MOE_RUN_CAPS)
MOE_ZROWS = 256
HALF = D_MODEL // 2
HI_MASK = -65536


def _pack_bf16_pairs(t):
    lo = lax.bitcast_convert_type(t[:, :HALF], jnp.int32)
    hi = lax.bitcast_convert_type(t[:, HALF:], jnp.int32)
    return (hi & jnp.int32(HI_MASK)) | lax.shift_right_logical(lo, jnp.int32(16))


def _unpack_bf16_pairs(w):
    lo = lax.bitcast_convert_type(lax.shift_left(w, jnp.int32(16)), F32)
    hi = lax.bitcast_convert_type(w & jnp.int32(HI_MASK), F32)
    return jnp.concatenate([lo, hi], axis=1).astype(BF16)


def _route_kernel(x_ref, rw_ref, rb_ref, slot_ref, gate_ref, slott_ref, meta_ref):
    tt = x_ref.shape[0]
    logits = jnp.dot(x_ref[...].astype(BF16), rw_ref[...], preferred_element_type=F32)
    scores = 1.0 / (1.0 + jnp.exp(-logits))
    lane = lax.broadcasted_iota(jnp.int32, logits.shape, 1)
    lane_f = lane.astype(F32)
    work = jnp.where(lane < N_EXPERTS, scores + rb_ref[...], NEG)
    sel = lane < 0
    firsts = []
    for _ in range(TOP_K):
        m = jnp.max(work, axis=1, keepdims=True)
        first = jnp.min(jnp.where(work == m, lane_f, float(LANES)), axis=1, keepdims=True)
        pick = lane_f == first
        sel = sel | pick
        work = jnp.where(pick, NEG, work)
        firsts.append(first)
    gs = jnp.where(sel, scores, 0.0)
    gates = gs / jnp.sum(gs, axis=1, keepdims=True) * ROUTED_SCALE

    sel_f = jnp.where(sel, 1.0, 0.0).astype(F32)
    r_i = lax.broadcasted_iota(jnp.int32, (tt, tt), 0)
    c_i = lax.broadcasted_iota(jnp.int32, (tt, tt), 1)
    below = jnp.where(r_i > c_i, 1.0, 0.0).astype(BF16)
    rank = jnp.dot(below, sel_f.astype(BF16), preferred_element_type=F32)
    cnt = jnp.sum(sel_f, axis=0, keepdims=True)
    cnt_pad = jnp.floor((cnt + (MOE_CH - 1)) * (1.0 / MOE_CH)) * MOE_CH
    e_r = lax.broadcasted_iota(jnp.int32, (LANES, LANES), 0)
    e_c = lax.broadcasted_iota(jnp.int32, (LANES, LANES), 1)
    before = jnp.where(e_r < e_c, 1.0, 0.0).astype(BF16)
    base = jnp.dot(jnp.broadcast_to(cnt_pad, (8, LANES)).astype(BF16), before,
                   preferred_element_type=F32)[0:1]
    slot = base + rank

    slot_tk = jnp.full(logits.shape, -1.0, F32)
    gate_tk = jnp.zeros(logits.shape, F32)
    for k, first in enumerate(firsts):
        pick = lane_f == first
        slot_k = jnp.sum(jnp.where(pick, slot, 0.0), axis=1, keepdims=True)
        gate_k = jnp.sum(jnp.where(pick, gates, 0.0), axis=1, keepdims=True)
        slot_tk = jnp.where(lane == k, slot_k, slot_tk)
        gate_tk = jnp.where(lane == k, gate_k, gate_tk)
    slot_ref[...] = slot_tk.astype(jnp.int32)
    gate_ref[...] = gate_tk
    slott_ref[0] = slot_tk.T[:TOP_K].astype(jnp.int32)
    row = lax.broadcasted_iota(jnp.int32, (8, LANES), 0)
    meta = jnp.where(row == 0, jnp.broadcast_to(cnt_pad, (8, LANES)), jnp.broadcast_to(base, (8, LANES)))
    meta_ref[0] = meta.astype(jnp.int32)


def _route(x, rw, rb):
    T, D = x.shape
    nt = T // MOE_TT
    return pl.pallas_call(
        _route_kernel,
        grid=(nt,),
        in_specs=[pl.BlockSpec((MOE_TT, D), lambda i: (i, 0)), pl.BlockSpec((D, LANES), lambda i: (0, 0)),
                  pl.BlockSpec((1, LANES), lambda i: (0, 0))],
        out_specs=[pl.BlockSpec((MOE_TT, LANES), lambda i: (i, 0)), pl.BlockSpec((MOE_TT, LANES), lambda i: (i, 0)),
                   pl.BlockSpec((1, TOP_K, MOE_TT), lambda i: (i, 0, 0)), pl.BlockSpec((1, 8, LANES), lambda i: (i, 0, 0))],
        out_shape=[jax.ShapeDtypeStruct((T, LANES), jnp.int32), jax.ShapeDtypeStruct((T, LANES), F32),
                   jax.ShapeDtypeStruct((nt, TOP_K, MOE_TT), jnp.int32), jax.ShapeDtypeStruct((nt, 8, LANES), jnp.int32)],
        compiler_params=_cparams("parallel"),
        name="moe_route",
    )(x, rw, rb)


def _chunk_copy(src_ref, dst_ref, sem):
    return pltpu.make_async_copy(src_ref, dst_ref, sem)


def _wait_n(n, wait_one):
    def eight(_, c):
        for _ in range(8):
            wait_one()
        return c

    def one(_, c):
        wait_one()
        return c
    lax.fori_loop(0, lax.shift_right_logical(n, 3), eight, 0)
    lax.fori_loop(0, n & 7, one, 0)


def _for_each_run(t, src_ref, dst_ref, num_ref, fn):
    off = 0
    for ci, (run, cap) in enumerate(zip(MOE_RUNS, MOE_RUN_CAPS)):
        def body(p, c, off=off, rows=run * MOE_CH):
            k = t * MOE_NRUN + off + p
            fn(rows, pl.multiple_of(src_ref[k], MOE_CH), pl.multiple_of(dst_ref[k], MOE_CH))
            return c
        lax.fori_loop(0, num_ref[t * len(MOE_RUNS) + ci], body, 0)
        off += cap


def _wait_runs(t, num_ref, wait_one):
    for ci, run in enumerate(MOE_RUNS):
        _wait_n(num_ref[t * len(MOE_RUNS) + ci], functools.partial(wait_one, run * MOE_CH))


def _dispatch_kernel(src_ref, dst_ref, num_ref, tail_start_ref, tail_n_ref,
                     x_ref, slott_ref, xs_ref, loc_ref, zero_ref, sem, tail_sem):
    i = pl.program_id(0)
    nt = pl.num_programs(0)
    cur = i % 2
    s_iota = lax.broadcasted_iota(jnp.int32, (MOE_SLOTS, MOE_TT), 0).astype(jnp.int16)
    slots = slott_ref[0].astype(jnp.int16)
    hit = s_iota == slots[0:1, :]
    for k in range(1, TOP_K):
        hit = hit | (s_iota == slots[k:k + 1, :])
    onehot = jnp.where(hit, jnp.ones((), BF16), jnp.zeros((), BF16))
    rows = jnp.dot(onehot, x_ref[...].astype(BF16), preferred_element_type=F32)
    loc_ref[cur] = _pack_bf16_pairs(rows)

    def run_copy(buf, rows, local, glob):
        return _chunk_copy(loc_ref.at[buf, pl.ds(local, rows), :], xs_ref.at[pl.ds(glob, rows), :], sem)

    @pl.when(i > 0)
    def _():
        _wait_runs(i - 1, num_ref, lambda rows: run_copy(0, rows, 0, 0).wait())

    _for_each_run(i, src_ref, dst_ref, num_ref, lambda rows, local, glob: run_copy(cur, rows, local, glob).start())

    @pl.when(i == nt - 1)
    def _():
        zero_ref[...] = jnp.zeros_like(zero_ref)

        def fill(rows_per_copy, first, last):
            z = zero_ref.at[pl.ds(0, rows_per_copy), :]

            def region(e, c):
                g = tail_start_ref[e]

                def start(j, c2):
                    row = pl.multiple_of(g + j * rows_per_copy, MOE_CH)
                    _chunk_copy(z, xs_ref.at[pl.ds(row, rows_per_copy), :], tail_sem).start()
                    return c2
                return lax.fori_loop(0, tail_n_ref[e], start, c)
            lax.fori_loop(first, last, region, 0)

            def region_wait(e, c):
                _wait_n(tail_n_ref[e], lambda: _chunk_copy(z, xs_ref.at[pl.ds(0, rows_per_copy), :],
                                                           tail_sem).wait())
                return c
            lax.fori_loop(first, last, region_wait, 0)

        fill(MOE_CH, 0, N_EXPERTS)
        fill(MOE_ZROWS, N_EXPERTS, N_EXPERTS + 1)
        _wait_runs(i, num_ref, lambda rows: run_copy(0, rows, 0, 0).wait())


def _dispatch(x, slot_kt, tables, n_rows):
    T, D = x.shape
    nt = T // MOE_TT
    src, dst, num, tail_start, tail_n = tables
    return pl.pallas_call(
        _dispatch_kernel,
        grid_spec=pltpu.PrefetchScalarGridSpec(
            num_scalar_prefetch=5,
            grid=(nt,),
            in_specs=[pl.BlockSpec((MOE_TT, D), lambda i, *_: (i, 0)),
                      pl.BlockSpec((1, TOP_K, MOE_TT), lambda i, *_: (i, 0, 0))],
            out_specs=pl.BlockSpec(memory_space=pl.ANY),
            scratch_shapes=[pltpu.VMEM((2, MOE_SLOTS, HALF), jnp.int32), pltpu.VMEM((MOE_ZROWS, HALF), jnp.int32),
                            pltpu.SemaphoreType.DMA(()), pltpu.SemaphoreType.DMA(())],
        ),
        out_shape=jax.ShapeDtypeStruct((n_rows, HALF), jnp.int32),
        compiler_params=_cparams("arbitrary"),
        name="moe_dispatch",
    )(src, dst, num, tail_start, tail_n, x, slot_kt)


def _expert_kernel(be_ref, nu_ref, xs_ref, w1_ref, w3_ref, w2_ref, ys_ref, w1b_ref, w3b_ref, w2b_ref):
    i = pl.program_id(0)

    @pl.when((i == 0) | (be_ref[i] != be_ref[jnp.maximum(i - 1, 0)]))
    def _():
        w1b_ref[...] = w1_ref[0].astype(BF16)
        w3b_ref[...] = w3_ref[0].astype(BF16)
        w2b_ref[...] = w2_ref[0].astype(BF16)

    @pl.when(i < nu_ref[0])
    def _():
        xb = _unpack_bf16_pairs(xs_ref[...])
        h1 = jnp.dot(xb, w1b_ref[...], preferred_element_type=F32)
        h3 = jnp.dot(xb, w3b_ref[...], preferred_element_type=F32)
        h = h1 / (1.0 + jnp.exp(-h1)) * h3
        y = jnp.dot(h.astype(BF16), w2b_ref[...], preferred_element_type=F32)
        ys_ref[...] = _pack_bf16_pairs(y.astype(BF16).astype(F32))

    @pl.when(i >= nu_ref[0])
    def _():
        ys_ref[...] = jnp.zeros_like(ys_ref)


def _experts(xs, block_expert, n_used, w1, w3, w2):
    n_rows = xs.shape[0]
    nb = n_rows // MOE_BM
    _, D, ff = w1.shape
    blk = lambda i, be, nu: (jnp.minimum(i, nu[0] - 1), 0)
    return pl.pallas_call(
        _expert_kernel,
        grid_spec=pltpu.PrefetchScalarGridSpec(
            num_scalar_prefetch=2,
            grid=(nb,),
            in_specs=[pl.BlockSpec((MOE_BM, HALF), blk),
                      pl.BlockSpec((1, D, ff), lambda i, be, nu: (be[i], 0, 0)),
                      pl.BlockSpec((1, D, ff), lambda i, be, nu: (be[i], 0, 0)),
                      pl.BlockSpec((1, ff, D), lambda i, be, nu: (be[i], 0, 0))],
            out_specs=pl.BlockSpec((MOE_BM, HALF), lambda i, be, nu: (i, 0)),
            scratch_shapes=[pltpu.VMEM((D, ff), BF16), pltpu.VMEM((D, ff), BF16), pltpu.VMEM((ff, D), BF16)],
        ),
        out_shape=jax.ShapeDtypeStruct((n_rows, HALF), jnp.int32),
        compiler_params=_cparams("arbitrary"),
        name="moe_experts",
    )(block_expert, n_used, xs, w1, w3, w2)


def _combine_kernel(src_ref, dst_ref, num_ref, slot_ref, gate_ref, ys_ref, o_ref, loc_ref, sem):
    i = pl.program_id(0)
    nt = pl.num_programs(0)
    cur = i % 2

    def run_copy(buf, rows, local, glob):
        return _chunk_copy(ys_ref.at[pl.ds(glob, rows), :], loc_ref.at[buf, pl.ds(local, rows), :], sem.at[buf])

    def fetch(t, buf):
        _for_each_run(t, src_ref, dst_ref, num_ref, lambda rows, local, glob: run_copy(buf, rows, local, glob).start())

    @pl.when(i == 0)
    def _():
        loc_ref[...] = jnp.zeros_like(loc_ref)
        fetch(0, 0)

    @pl.when(i + 1 < nt)
    def _():
        fetch(i + 1, 1 - cur)

    _wait_runs(i, num_ref, lambda rows: run_copy(cur, rows, 0, 0).wait())

    s_iota = lax.broadcasted_iota(jnp.int32, (MOE_TT, MOE_SLOTS), 1)
    slots = slot_ref[...]
    gates = gate_ref[...]
    wgt = jnp.zeros((MOE_TT, MOE_SLOTS), F32)
    for k in range(TOP_K):
        wgt = jnp.where(s_iota == slots[:, k:k + 1], gates[:, k:k + 1], wgt)
    w_hi = wgt.astype(BF16)
    w_lo = (wgt - w_hi.astype(F32)).astype(BF16)
    ys = _unpack_bf16_pairs(loc_ref[cur])
    o_ref[...] = (jnp.dot(w_hi, ys, preferred_element_type=F32) + jnp.dot(w_lo, ys, preferred_element_type=F32))


def _combine(ys, slot_tk, gate_tk, tables):
    T = slot_tk.shape[0]
    nt = T // MOE_TT
    src, dst, num = tables
    return pl.pallas_call(
        _combine_kernel,
        grid_spec=pltpu.PrefetchScalarGridSpec(
            num_scalar_prefetch=3,
            grid=(nt,),
            in_specs=[pl.BlockSpec((MOE_TT, LANES), lambda i, *_: (i, 0)),
                      pl.BlockSpec((MOE_TT, LANES), lambda i, *_: (i, 0)),
                      pl.BlockSpec(memory_space=pl.ANY)],
            out_specs=pl.BlockSpec((MOE_TT, D_MODEL), lambda i, *_: (i, 0)),
            scratch_shapes=[pltpu.VMEM((2, MOE_SLOTS, HALF), jnp.int32), pltpu.SemaphoreType.DMA((2,))],
        ),
        out_shape=jax.ShapeDtypeStruct((T, D_MODEL), F32),
        compiler_params=_cparams("arbitrary"),
        name="moe_combine",
    )(src, dst, num, slot_tk, gate_tk, ys)


def _shared_ln_kernel(x_ref, r_ref, w1_ref, w3_ref, w2_ref, g_ref, b_ref, o_ref, *, alpha):
    x = x_ref[...]
    xb = x.astype(BF16)
    h1 = jnp.dot(xb, w1_ref[...], preferred_element_type=F32)
    h3 = jnp.dot(xb, w3_ref[...], preferred_element_type=F32)
    h = h1 / (1.0 + jnp.exp(-h1)) * h3
    y = jnp.dot(h.astype(BF16), w2_ref[...], preferred_element_type=F32)
    o_ref[...] = _layer_norm(alpha * x + (r_ref[...] + y), g_ref[...], b_ref[...])


def _shared_ln(x, routed, w1, w3, w2, g, b, alpha, tm=512):
    T, D = x.shape
    row = lambda i: (i, 0)
    fix = lambda i: (0, 0)
    return pl.pallas_call(
        functools.partial(_shared_ln_kernel, alpha=alpha),
        grid=(T // tm,),
        in_specs=[pl.BlockSpec((tm, D), row), pl.BlockSpec((tm, D), row), pl.BlockSpec(w1.shape, fix),
                  pl.BlockSpec(w3.shape, fix), pl.BlockSpec(w2.shape, fix), pl.BlockSpec((1, D), fix),
                  pl.BlockSpec((1, D), fix)],
        out_specs=pl.BlockSpec((tm, D), row),
        out_shape=jax.ShapeDtypeStruct((T, D), F32),
        compiler_params=_cparams("parallel"),
        name="shared_ffn_ln",
    )(x, routed, w1, w3, w2, g, b)


def _dispatch_tables(meta, n_blocks):
    cnt = meta[:, 0, :N_EXPERTS]
    base = meta[:, 1, :N_EXPERTS]
    tot = jnp.sum(cnt, axis=0)
    nblk = (tot + MOE_BM - 1) // MOE_BM
    bend = jnp.cumsum(nblk)
    pstart = (bend - nblk) * MOE_BM
    gstart = pstart[None, :] + jnp.cumsum(cnt, axis=0) - cnt
    n_used = bend[-1:]
    blocks = jnp.arange(n_blocks, dtype=jnp.int32)
    be = jnp.sum((jnp.minimum(blocks, n_used - 1)[:, None] >= bend[None, :]).astype(jnp.int32), axis=1)
    chunks = cnt // MOE_CH
    done = jnp.zeros_like(chunks)
    src_l, dst_l, num_l = [], [], []
    for run, cap in zip(MOE_RUNS, MOE_RUN_CAPS):
        m = (chunks - done) // run
        cum = jnp.cumsum(m, axis=1) - m
        p = jnp.arange(cap, dtype=jnp.int32)
        owned = cum[:, None, :] <= p[None, :, None]
        last = owned & ~jnp.concatenate([owned[:, :, 1:], jnp.zeros_like(owned[:, :, :1])], axis=-1)
        pick = lambda t: jnp.sum(jnp.where(last, t[:, None, :], 0), axis=-1)
        within = (p[None, :] - pick(cum)) * (run * MOE_CH) + pick(done) * MOE_CH
        src_l.append(pick(base) + within)
        dst_l.append(pick(gstart) + within)
        num_l.append(jnp.sum(m, axis=1))
        done = done + m * run
    i32 = lambda a: a.astype(jnp.int32)
    common = (i32(jnp.concatenate(src_l, axis=1).reshape(-1)), i32(jnp.concatenate(dst_l, axis=1).reshape(-1)),
              i32(jnp.stack(num_l, axis=1).reshape(-1)))
    tails = (i32(jnp.concatenate([pstart + tot, n_used * MOE_BM])),
             i32(jnp.concatenate([(nblk * MOE_BM - tot) // MOE_CH, (n_blocks - n_used) * (MOE_BM // MOE_ZROWS)])))
    return common, tails, i32(jnp.minimum(be, N_EXPERTS - 1)), i32(n_used)


def _moe_block(x1, rw, rb, w1, w3, w2, sw1, sw3, sw2, lg, lb, alpha):
    T = x1.shape[0]
    nt = T // MOE_TT
    n_blocks = -(-(TOP_K * T + nt * N_EXPERTS * (MOE_CH - 1)) // MOE_BM) + N_EXPERTS
    slot_tk, gate_tk, slot_kt, meta = _route(x1, rw, rb)
    common, tails, be, n_used = _dispatch_tables(meta, n_blocks)
    xs = _dispatch(x1, slot_kt, common + tails, n_blocks * MOE_BM)
    ys = _experts(xs, be, n_used, w1, w3, w2)
    routed = _combine(ys, slot_tk, gate_tk, common)
    return _shared_ln(x1, routed, sw1, sw3, sw2, lg, lb, alpha)


def _t5_bucket(dist):
    max_exact = N_BUCKETS // 2
    d_f = jnp.maximum(dist, 1).astype(F32)
    large = max_exact + (jnp.log(d_f / max_exact) / math.log(MAX_DISTANCE / max_exact)
                         * (N_BUCKETS - max_exact)).astype(jnp.int32)
    large = jnp.minimum(large, N_BUCKETS - 1)
    return jnp.where(dist < max_exact, dist, large)


def _bias_tables(rel_bias, S):
    dist = np.arange(S)
    e = rel_bias[_t5_bucket(jnp.asarray(dist, jnp.int32))].T
    mult = sum(((dist % d == 0) & (dist <= w)).astype(np.float32) for w, d in DILATIONS)
    e_dil = jnp.where(mult > 0, e[:N_HEADS_DIL] + jnp.log(jnp.maximum(mult, 1.0)), NEG)
    e_all = jnp.concatenate([e_dil, e[N_HEADS_DIL:], jnp.zeros((1, S), F32)], 0)
    n_h = e_all.shape[0]
    a = jnp.concatenate([e_all[:, :S - BQ + 1][:, ::-1], jnp.full((n_h, BQ), NEG, F32),
                         e_all[:, S - BQ + 1:][:, ::-1]], 1)
    w = _toeplitz(a.astype(F32), S)
    return w[:N_HEADS_DIL], w[N_HEADS_DIL:N_HEADS_DIL + N_HEADS_DSA], w[-1:]


def _toeplitz_kernel(a_ref, o_ref):
    period = a_ref.shape[-1]
    rolled = pltpu.roll(jnp.broadcast_to(a_ref[0], (BQ, period)), 0, 1, stride=1, stride_axis=0)
    o_ref[0] = rolled[:, :o_ref.shape[-1]]


def _toeplitz(a, S):
    n_h, period = a.shape
    return pl.pallas_call(
        _toeplitz_kernel,
        grid=(n_h,),
        in_specs=[pl.BlockSpec((1, 1, period), lambda h: (h, 0, 0))],
        out_specs=pl.BlockSpec((1, BQ, S), lambda h: (h, 0, 0)),
        out_shape=jax.ShapeDtypeStruct((n_h, BQ, S), F32),
        compiler_params=_cparams("parallel"),
        name="bias_toeplitz",
    )(a.reshape(n_h, 1, period))


def _rope_tables(positions):
    inv = ROPE_THETA ** (-jnp.arange(0, MLA_ROPE, 2, dtype=F32) / MLA_ROPE)
    ang = positions.astype(F32).reshape(-1, 1) * inv
    cos, sin = jnp.cos(ang), jnp.sin(ang)
    T, half = cos.shape
    one = jnp.ones((T, MLA_NOPE), F32)
    z = lambda n: jnp.zeros((T, n), F32)
    tail = LANES - MLA_NOPE - MLA_ROPE
    c = jnp.concatenate([one, cos, cos, jnp.ones((T, tail), F32)], 1)
    s1 = jnp.concatenate([z(MLA_NOPE + half), sin, z(tail)], 1)
    s2 = jnp.concatenate([z(MLA_NOPE), -sin, z(half + tail)], 1)
    return c, s1, s2


def _split_w_in(w):
    o = np.cumsum([0, 384, 384, 384, 256, 256, 256, 512, 64, 8, 256, 128, 32])
    ik = w[:, o[7]:o[8]]
    wa = jnp.concatenate([w[:, o[6]:o[7]], w[:, o[3]:o[6]], ik, ik, w[:, :o[3]]], 1)
    z = lambda n: jnp.zeros((w.shape[0], n), w.dtype)
    chunk = jnp.concatenate([w[:, o[8]:o[9]], z(MLA_NOPE - IDX_HEADS), w[:, o[11]:o[12]],
                             z(LANES - MLA_NOPE - MLA_ROPE)], 1)
    wb = jnp.concatenate([w[:, o[9]:o[11]], chunk], 1)
    return wa.astype(BF16), wb.astype(BF16)


def _split_mla_weights(w_uq, w_ukv):
    r = w_uq.shape[0]
    wq = w_uq.reshape(r, N_HEADS_MLA, MLA_NOPE + MLA_ROPE)
    wq = jnp.pad(wq, ((0, 0), (0, 0), (0, LANES - MLA_NOPE - MLA_ROPE))).reshape(r, N_HEADS_MLA * LANES)
    r = w_ukv.shape[0]
    wkv = w_ukv.reshape(r, N_HEADS_MLA, MLA_NOPE + MLA_V)
    wk = jnp.pad(wkv[:, :, :MLA_NOPE], ((0, 0), (0, 0), (0, LANES - MLA_NOPE))).reshape(r, N_HEADS_MLA * LANES)
    wv = wkv[:, :, MLA_NOPE:].reshape(r, N_HEADS_MLA * MLA_V)
    return wq.astype(BF16), wk.astype(BF16), wv.astype(BF16)


def kernel(x, positions, w_in, mla_q_norm, mla_w_uq, mla_kv_norm, mla_w_ukv, w_out, ln1_g, ln1_b, router_w,
           router_bias, exp_w1, exp_w3, exp_w2, sh_w1, sh_w3, sh_w2, ln2_g, ln2_b, rel_bias):
    B, S, D = x.shape
    T = B * S
    depth = w_in.shape[0]
    alpha = (2 * depth) ** 0.25
    w_dil, w_dsa, w_causal = _bias_tables(rel_bias, S)
    rope_c, rope_s1, rope_s2 = _rope_tables(positions)
    na, nb = N_HEADS_DIL * HEAD_DIM, (N_HEADS_DIL + N_HEADS_DSA) * HEAD_DIM
    xf = x.reshape(T, D)
    for l in range(depth):
        wa, wb = _split_w_in(w_in[l])
        ua = _matmul(xf, wa, BF16)
        ub = _matmul(xf, wb, F32)
        ua3 = ua.reshape(B, S, WA_COLS)
        o_a = _attention(ua3, ua3, ua3, w_dil, q_blk=DQ_BLK, k_blk=DK_BLK, v_blk=DV_BLK,
                         n_pairs=N_HEADS_DIL // 2, packed=True, per_head_w=True, name="dilated_attn")
        o_b = _dsa(ua3, ub.reshape(B, S, WB_COLS), w_dsa)
        wq, wk, wv = _split_mla_weights(mla_w_uq[l], mla_w_ukv[l])
        q, k, v = _mla_prep(ub, mla_q_norm[l][None], mla_kv_norm[l][None], wq, wk, wv, rope_c, rope_s1, rope_s2)
        o_c = _attention(q.reshape(B, S, -1), k.reshape(B, S, -1), v.reshape(B, S, -1), w_causal,
                         q_blk=0, k_blk=0, v_blk=0, n_pairs=N_HEADS_MLA // 2, packed=False, per_head_w=False,
                         name="mla_attn")
        wo = w_out[l].astype(BF16)
        x1 = _outproj_ln(o_a.reshape(T, -1), o_b.reshape(T, -1), o_c.reshape(T, -1),
                         wo[:na], wo[na:nb], wo[nb:], xf, ln1_g[l][None], ln1_b[l][None], alpha)
        rw = jnp.pad(router_w[l], ((0, 0), (0, LANES - N_EXPERTS))).astype(BF16)
        rb = jnp.pad(router_bias[l].astype(F32), (0, LANES - N_EXPERTS))[None]
        xf = _moe_block(x1, rw, rb, exp_w1[l], exp_w3[l], exp_w2[l], sh_w1[l].astype(BF16), sh_w3[l].astype(BF16),
                        sh_w2[l].astype(BF16), ln2_g[l][None], ln2_b[l][None], alpha)
    return xf.reshape(B, S, D)
```
